```python
import jax, jax.numpy as jnp
from jax import lax
import numpy as np

D_MODEL = 2048
BATCH = 1
SEQ = 16384
DEPTH = 1
DEC_BATCH = 2
DEC_SEQ = 8192
PAST_LEN = 128

HEAD_DIM = 128
A_HEADS = 8
A_KV_HEADS = 2
B_HEADS = 4
B_KV_HEADS = 2
M_HEADS = 4
N_MEM = 256
WINDOW = 128
BLOCK = 128
GRID_W = 64
ROPE_THETA = 10000.0
NORM_EPS = 1e-6
D_FF = ((8 * D_MODEL // 3 + 255) // 256) * 256

A_Q = A_HEADS * HEAD_DIM
A_KV = A_KV_HEADS * HEAD_DIM
B_Q = B_HEADS * HEAD_DIM
B_KV = B_KV_HEADS * HEAD_DIM
M_Q = M_HEADS * HEAD_DIM
MIX_WIDTH = A_Q + B_Q + M_Q
IN_WIDTH = A_Q + 2 * A_KV + B_Q + 2 * B_KV + M_Q
MEM_KV_WIDTH = 2 * M_Q
NEG_INF = -1e30

kernel_name = "hymba_window_axial_memory_encoder"


def rmsnorm(x, g):
    x32 = x.astype(jnp.float32)
    y = x32 * lax.rsqrt(jnp.mean(x32 * x32, axis=-1, keepdims=True) + NORM_EPS)
    return (y * g.astype(jnp.float32)).astype(x.dtype)


def rope_tables(pos, dim):
    inv = ROPE_THETA ** (-(jnp.arange(0, dim, 2, dtype=jnp.float32) / dim))
    ang = pos[:, None] * inv[None, :]
    return jnp.cos(ang), jnp.sin(ang)


def apply_rope(x, cos, sin):
    x32 = x.astype(jnp.float32)
    half = x.shape[-1] // 2
    x1, x2 = x32[..., :half], x32[..., half:]
    c, s = cos[None, :, None, :], sin[None, :, None, :]
    return jnp.concatenate([x1 * c - x2 * s, x2 * c + x1 * s], axis=-1).astype(x.dtype)


def window_attention(q, k, v, sink):
    B, S, H, D = q.shape
    KV = k.shape[2]
    G = H // KV
    nb = S // BLOCK
    qb = q.reshape(B, nb, BLOCK, KV, G, D)

    def band(t):
        tp = jnp.pad(t, ((0, 0), (BLOCK, BLOCK), (0, 0), (0, 0)))
        parts = [tp[:, i * BLOCK:i * BLOCK + S].reshape(B, nb, BLOCK, KV, D) for i in range(3)]
        return jnp.concatenate(parts, axis=2)

    kb, vb = band(k), band(v)
    logits = jnp.einsum('bnqkgd,bnskd->bnkgqs', qb, kb).astype(jnp.float32) * (D ** -0.5)
    n_idx = jnp.arange(nb)[:, None, None]
    q_pos = n_idx * BLOCK + jnp.arange(BLOCK)[None, :, None]
    k_pos = n_idx * BLOCK + jnp.arange(3 * BLOCK)[None, None, :] - BLOCK
    mask = (jnp.abs(k_pos - q_pos) <= WINDOW) & (k_pos >= 0) & (k_pos < S)
    logits = jnp.where(mask[None, :, None, None], logits, NEG_INF)
    sink_col = jnp.broadcast_to(sink.astype(jnp.float32).reshape(KV, G)[None, None, :, :, None, None],
                                logits.shape[:-1] + (1,))
    p = jax.nn.softmax(jnp.concatenate([logits, sink_col], axis=-1), axis=-1)[..., :-1]
    o = jnp.einsum('bnkgqs,bnskd->bnqkgd', p.astype(v.dtype), vb)
    return o.reshape(B, S, H * D)


def global_attention(q, k, v):
    B, S, H, D = q.shape
    KV = k.shape[2]
    G = H // KV
    nb = S // BLOCK
    qb = q.reshape(B, nb, BLOCK, KV, G, D).transpose(1, 0, 2, 3, 4, 5)

    def one_block(qi):
        s = jnp.einsum('bqkgd,bskd->bkgqs', qi, k).astype(jnp.float32) * (D ** -0.5)
        p = jax.nn.softmax(s, axis=-1).astype(v.dtype)
        return jnp.einsum('bkgqs,bskd->bqkgd', p, v)

    o = lax.map(one_block, qb)
    return o.transpose(1, 0, 2, 3, 4, 5).reshape(B, S, H * D)


def memory_attention(q, km, vm):
    D = q.shape[-1]
    s = jnp.einsum('bqhd,bmhd->bhqm', q, km).astype(jnp.float32) * (D ** -0.5)
    p = jax.nn.softmax(s, axis=-1).astype(vm.dtype)
    o = jnp.einsum('bhqm,bmhd->bqhd', p, vm)
    return o.reshape(q.shape[0], q.shape[1], -1)


def encoder_layer(x, mem, norm_mix_g, norm_mem_g, w_in, w_mem_kv, sink_a, q_norm_b_g, k_norm_b_g,
                  out_norm_g, w_out, norm_ffn_g, w_gate_up, w_down):
    B, S, _ = x.shape
    ROWS = S // GRID_W
    h = rmsnorm(x, norm_mix_g)
    proj = h @ w_in
    offs = np.cumsum([A_Q, A_KV, A_KV, B_Q, B_KV, B_KV])
    qa, ka, va, qb, kb, vb, qm = jnp.split(proj, offs, axis=-1)
    qa = qa.reshape(B, S, A_HEADS, HEAD_DIM)
    ka = ka.reshape(B, S, A_KV_HEADS, HEAD_DIM)
    va = va.reshape(B, S, A_KV_HEADS, HEAD_DIM)
    qb = qb.reshape(B, S, B_HEADS, HEAD_DIM)
    kb = kb.reshape(B, S, B_KV_HEADS, HEAD_DIM)
    vb = vb.reshape(B, S, B_KV_HEADS, HEAD_DIM)
    qm = qm.reshape(B, S, M_HEADS, HEAD_DIM)

    cos_t, sin_t = rope_tables(jnp.arange(S, dtype=jnp.float32), HEAD_DIM)
    oa = window_attention(apply_rope(qa, cos_t, sin_t), apply_rope(ka, cos_t, sin_t), va, sink_a)

    rows = jnp.repeat(jnp.arange(ROWS, dtype=jnp.float32), GRID_W, total_repeat_length=S)
    cols = jnp.tile(jnp.arange(GRID_W, dtype=jnp.float32), ROWS)
    half = HEAD_DIM // 2
    cos_r, sin_r = rope_tables(rows, half)
    cos_c, sin_c = rope_tables(cols, half)

    def axial(t):
        return jnp.concatenate([apply_rope(t[..., :half], cos_r, sin_r),
                                apply_rope(t[..., half:], cos_c, sin_c)], axis=-1)

    qb = axial(rmsnorm(qb, q_norm_b_g))
    kb = axial(rmsnorm(kb, k_norm_b_g))
    ob = global_attention(qb, kb, vb)

    kvm = rmsnorm(mem, norm_mem_g) @ w_mem_kv
    km, vm = jnp.split(kvm, 2, axis=-1)
    km = km.reshape(B, N_MEM, M_HEADS, HEAD_DIM)
    vm = vm.reshape(B, N_MEM, M_HEADS, HEAD_DIM)
    om = memory_attention(qm, km, vm)

    merged = jnp.concatenate([rmsnorm(oa, out_norm_g[:A_Q]),
                              rmsnorm(ob, out_norm_g[A_Q:A_Q + B_Q]),
                              rmsnorm(om, out_norm_g[A_Q + B_Q:])], axis=-1)
    x = x + merged @ w_out

    g, u = jnp.split(rmsnorm(x, norm_ffn_g) @ w_gate_up, 2, axis=-1)
    return x + (jax.nn.silu(g) * u) @ w_down


def trunk(x, mem, norm_mix_g, norm_mem_g, w_in, w_mem_kv, sink_a, q_norm_b_g, k_norm_b_g,
          out_norm_g, w_out, norm_ffn_g, w_gate_up, w_down, norm_final_g):
    for l in range(DEPTH):
        x = encoder_layer(x, mem, norm_mix_g[l], norm_mem_g[l], w_in[l], w_mem_kv[l], sink_a[l],
                          q_norm_b_g[l], k_norm_b_g[l], out_norm_g[l], w_out[l], norm_ffn_g[l],
                          w_gate_up[l], w_down[l])
    return rmsnorm(x, norm_final_g)


def setup_inputs(seed: int = 0) -> dict:
    key = jax.random.key(seed)
    ks = jax.random.split(key, 20)
    f32 = jnp.float32

    def normal(k, shape, scale):
        return jax.random.normal(k, shape, f32) * scale

    def gain(k, shape):
        return 1.0 + normal(k, shape, 0.02)

    return {
        "x_prompt": normal(ks[0], (BATCH, SEQ, D_MODEL), 1.0),
        "x_sample": normal(ks[1], (DEC_BATCH, DEC_SEQ, D_MODEL), 1.0),
        "mem_prompt": normal(ks[2], (BATCH, N_MEM, D_MODEL), 1.0),
        "mem_sample": normal(ks[3], (DEC_BATCH, N_MEM, D_MODEL), 1.0),
        "norm_mix_g": gain(ks[4], (DEPTH, D_MODEL)),
        "norm_mem_g": gain(ks[5], (DEPTH, D_MODEL)),
        "w_in": normal(ks[6], (DEPTH, D_MODEL, IN_WIDTH), D_MODEL ** -0.5),
        "w_mem_kv": normal(ks[7], (DEPTH, D_MODEL, MEM_KV_WIDTH), D_MODEL ** -0.5),
        "sink_a": normal(ks[8], (DEPTH, A_HEADS), 0.5),
        "q_norm_b_g": gain(ks[9], (DEPTH, HEAD_DIM)),
        "k_norm_b_g": gain(ks[10], (DEPTH, HEAD_DIM)),
        "out_norm_g": gain(ks[11], (DEPTH, MIX_WIDTH)),
        "w_out": normal(ks[12], (DEPTH, MIX_WIDTH, D_MODEL), MIX_WIDTH ** -0.5),
        "norm_ffn_g": gain(ks[13], (DEPTH, D_MODEL)),
        "w_gate_up": normal(ks[14], (DEPTH, D_MODEL, 2 * D_FF), D_MODEL ** -0.5),
        "w_down": normal(ks[15], (DEPTH, D_FF, D_MODEL), D_FF ** -0.5),
        "norm_final_g": gain(ks[16], (D_MODEL,)),
    }


def reference(x_prompt, x_sample, mem_prompt, mem_sample, norm_mix_g, norm_mem_g, w_in, w_mem_kv,
              sink_a, q_norm_b_g, k_norm_b_g, out_norm_g, w_out, norm_ffn_g, w_gate_up, w_down,
              norm_final_g):
    y_prompt = trunk(x_prompt, mem_prompt, norm_mix_g, norm_mem_g, w_in, w_mem_kv, sink_a, q_norm_b_g,
                     k_norm_b_g, out_norm_g, w_out, norm_ffn_g, w_gate_up, w_down, norm_final_g)
    y_sample = trunk(x_sample, mem_sample, norm_mix_g, norm_mem_g, w_in, w_mem_kv, sink_a, q_norm_b_g,
                     k_norm_b_g, out_norm_g, w_out, norm_ffn_g, w_gate_up, w_down, norm_final_g)
    return (y_prompt, y_sample)
```

```python
import functools

import jax
import jax.numpy as jnp
import numpy as np
from jax import lax
from jax.experimental import pallas as pl
from jax.experimental.pallas import tpu as pltpu

HEAD_DIM = 128
A_HEADS, A_KV_HEADS = 8, 2
B_HEADS, B_KV_HEADS = 4, 2
M_HEADS = 4
WINDOW = 128
GRID_W = 64
ROPE_THETA = 10000.0
NORM_EPS = 1e-6
NEG_INF = -1e30
A_Q, A_KV = A_HEADS * HEAD_DIM, A_KV_HEADS * HEAD_DIM
B_Q, B_KV = B_HEADS * HEAD_DIM, B_KV_HEADS * HEAD_DIM
M_Q = M_HEADS * HEAD_DIM
LOGIT_SCALE = HEAD_DIM ** -0.5

BF16 = jnp.bfloat16
F32 = jnp.float32

VMEM_LIMIT_BYTES = 56 * 1024 * 1024


def _params(*sem):
    return pltpu.CompilerParams(dimension_semantics=sem, vmem_limit_bytes=VMEM_LIMIT_BYTES)


def _resident(shape, index_map):
    return pl.BlockSpec(shape, index_map, pipeline_mode=pl.Buffered(1))


def _rms(x, g):
    ms = jnp.mean(x * x, axis=-1, keepdims=True)
    return x * lax.rsqrt(ms + NORM_EPS) * g


def _dot(a, b):
    return jnp.dot(a, b, preferred_element_type=F32)


def _dot_nt(a, b):
    return lax.dot_general(a, b, (((1,), (1,)), ((), ())), preferred_element_type=F32)


def _inproj_kernel(x_ref, g_ref, w_ref, ca_ref, sa_ref, cb_ref, sb_ref, qg_ref, kg_ref,
                   qa_ref, ka_ref, va_ref, qb_ref, kb_ref, vb_ref, qm_ref):
    h = _rms(x_ref[...], g_ref[...]).astype(BF16)
    ca, sa, cb, sb = ca_ref[...], sa_ref[...], cb_ref[...], sb_ref[...]
    lane = lax.broadcasted_iota(jnp.int32, ca.shape, 1)
    low_quarter = (lane % (HEAD_DIM // 2)) < (HEAD_DIM // 4)

    def rope_a(t):
        return t * ca + pltpu.roll(t, HEAD_DIM // 2, 1) * sa

    def rope_b(t):
        partner = jnp.where(low_quarter, pltpu.roll(t, HEAD_DIM - HEAD_DIM // 4, 1),
                            pltpu.roll(t, HEAD_DIM // 4, 1))
        return t * cb + partner * sb

    def heads(t, n, fn, out_ref):
        for i in range(n):
            sl = slice(i * HEAD_DIM, (i + 1) * HEAD_DIM)
            out_ref[:, sl] = fn(t[:, sl]).astype(out_ref.dtype)

    off = 0

    def section(width):
        nonlocal off
        t = _dot(h, w_ref[:, off:off + width])
        off += width
        return t

    heads(section(A_Q), A_HEADS, lambda t: rope_a(t) * LOGIT_SCALE, qa_ref)
    heads(section(A_KV), A_KV_HEADS, rope_a, ka_ref)
    va_ref[...] = section(A_KV).astype(va_ref.dtype)
    qg, kg = qg_ref[...], kg_ref[...]
    heads(section(B_Q), B_HEADS, lambda t: rope_b(_rms(t, qg)) * LOGIT_SCALE, qb_ref)
    heads(section(B_KV), B_KV_HEADS, lambda t: rope_b(_rms(t, kg)), kb_ref)
    vb_ref[...] = section(B_KV).astype(vb_ref.dtype)
    qm_ref[...] = (section(M_Q) * LOGIT_SCALE).astype(qm_ref.dtype)


def _rope_tables(seq):
    def tables(pos, dim):
        inv = ROPE_THETA ** (-(jnp.arange(0, dim, 2, dtype=F32) / dim))
        ang = pos[:, None] * inv[None, :]
        return jnp.cos(ang), jnp.sin(ang)

    t = jnp.arange(seq, dtype=jnp.int32)
    c, s = tables(t.astype(F32), HEAD_DIM)
    ca = jnp.concatenate([c, c], axis=-1)
    sa = jnp.concatenate([-s, s], axis=-1)
    rows = (t // GRID_W).astype(F32)
    cols = (t % GRID_W).astype(F32)
    cr, sr = tables(rows, HEAD_DIM // 2)
    cc, sc = tables(cols, HEAD_DIM // 2)
    cb = jnp.concatenate([cr, cr, cc, cc], axis=-1)
    sb = jnp.concatenate([-sr, sr, -sc, sc], axis=-1)
    return ca, sa, cb, sb


def _inproj(x2, seq, g, w_bf, qg, kg, tm):
    tokens, d = x2.shape
    per_seq = seq // tm
    ca, sa, cb, sb = _rope_tables(seq)
    row = lambda i: (i, 0)
    fixed = lambda i: (0, 0)
    tab = lambda i: (i % per_seq, 0)
    widths = (A_Q, A_KV, A_KV, B_Q, B_KV, B_KV, M_Q)
    return pl.pallas_call(
        _inproj_kernel,
        grid=(tokens // tm,),
        in_specs=[pl.BlockSpec((tm, d), row),
                  _resident((1, d), fixed),
                  _resident(w_bf.shape, fixed),
                  pl.BlockSpec((tm, HEAD_DIM), tab), pl.BlockSpec((tm, HEAD_DIM), tab),
                  pl.BlockSpec((tm, HEAD_DIM), tab), pl.BlockSpec((tm, HEAD_DIM), tab),
                  _resident((1, HEAD_DIM), fixed), _resident((1, HEAD_DIM), fixed)],
        out_specs=[pl.BlockSpec((tm, w), row) for w in widths],
        out_shape=[jax.ShapeDtypeStruct((tokens, w), BF16) for w in widths],
        compiler_params=_params("parallel"),
        name="inproj",
    )(x2, g.reshape(1, d), w_bf, ca, sa, cb, sb, qg.reshape(1, HEAD_DIM), kg.reshape(1, HEAD_DIM))


def _memkv_kernel(mem_ref, g_ref, w_ref, km_ref, vm_ref):
    h = _rms(mem_ref[...], g_ref[...]).astype(BF16)
    kv = _dot(h, w_ref[...])
    km_ref[...] = kv[:, :M_Q].astype(km_ref.dtype)
    vm_ref[...] = kv[:, M_Q:].astype(vm_ref.dtype)


def _memkv(mem, g, w_bf):
    b, n, d = mem.shape
    blk = lambda i: (i, 0, 0)
    fixed = lambda i: (0, 0)
    return pl.pallas_call(
        _memkv_kernel,
        grid=(b,),
        in_specs=[pl.BlockSpec((None, n, d), blk), _resident((1, d), fixed), _resident(w_bf.shape, fixed)],
        out_specs=[pl.BlockSpec((None, n, M_Q), blk), pl.BlockSpec((None, n, M_Q), blk)],
        out_shape=[jax.ShapeDtypeStruct((b, n, M_Q), BF16)] * 2,
        compiler_params=_params("parallel"),
        name="memkv",
    )(mem, g.reshape(1, d), w_bf)


def _local_kernel(sink_ref, qa_ref, kp_ref, kc_ref, kn_ref, vp_ref, vc_ref, vn_ref,
                  qm_ref, km_ref, vm_ref, ga_ref, gm_ref, oa_ref, om_ref, kbuf, vbuf, obuf, mbuf):
    bq = qa_ref.shape[0]
    nsub = bq // WINDOW
    i = pl.program_id(1)
    first = i == 0
    last = i == pl.num_programs(1) - 1

    kbuf[0:WINDOW] = kp_ref[...]
    kbuf[WINDOW:WINDOW + bq] = kc_ref[...]
    kbuf[WINDOW + bq:] = kn_ref[...]
    vbuf[0:WINDOW] = vp_ref[...]
    vbuf[WINDOW:WINDOW + bq] = vc_ref[...]
    vbuf[WINDOW + bq:] = vn_ref[...]

    group = A_HEADS // A_KV_HEADS
    rows, cols = group * WINDOW, 3 * WINDOW
    qpos = lax.broadcasted_iota(jnp.int32, (rows, cols), 0) % WINDOW
    kcol = lax.broadcasted_iota(jnp.int32, (rows, cols), 1)
    band = jnp.abs(kcol - WINDOW - qpos) <= WINDOW
    lo = jnp.where(first, WINDOW, 0)
    hi = jnp.where(last, 2 * WINDOW, 3 * WINDOW)

    for j in range(nsub):
        valid = band
        if j == 0:
            valid = valid & (kcol >= lo)
        if j == nsub - 1:
            valid = valid & (kcol < hi)
        for g in range(A_KV_HEADS):
            q = jnp.concatenate(
                [qa_ref[j * WINDOW:(j + 1) * WINDOW, (g * group + h) * HEAD_DIM:(g * group + h + 1) * HEAD_DIM]
                 for h in range(group)], axis=0)
            k = kbuf[j * WINDOW:j * WINDOW + cols, g * HEAD_DIM:(g + 1) * HEAD_DIM]
            v = vbuf[j * WINDOW:j * WINDOW + cols, g * HEAD_DIM:(g + 1) * HEAD_DIM]
            s = jnp.where(valid, _dot_nt(q, k), NEG_INF)
            sink = jnp.concatenate(
                [jnp.full((WINDOW, 1), sink_ref[g * group + h], F32) for h in range(group)], axis=0)
            m = jnp.maximum(jnp.max(s, axis=-1, keepdims=True), sink)
            p = jnp.exp(s - m)
            denom = jnp.sum(p, axis=-1, keepdims=True) + jnp.exp(sink - m)
            o = _dot(p.astype(BF16), v) * (1.0 / denom)
            for h in range(group):
                c0 = (g * group + h) * HEAD_DIM
                obuf[j * WINDOW:(j + 1) * WINDOW, c0:c0 + HEAD_DIM] = o[h * WINDOW:(h + 1) * WINDOW]
    oa_ref[...] = _rms(obuf[...], ga_ref[...]).astype(oa_ref.dtype)

    for h in range(M_HEADS):
        sl = slice(h * HEAD_DIM, (h + 1) * HEAD_DIM)
        s = _dot_nt(qm_ref[:, sl], km_ref[:, sl])
        m = jnp.max(s, axis=-1, keepdims=True)
        p = jnp.exp(s - m)
        denom = jnp.sum(p, axis=-1, keepdims=True)
        mbuf[:, sl] = _dot(p.astype(BF16), vm_ref[:, sl]) * (1.0 / denom)
    om_ref[...] = _rms(mbuf[...], gm_ref[...]).astype(om_ref.dtype)


def _local_attention(qa, ka, va, qm, km, vm, sink, ga, gm, bq):
    b, seq, _ = qa.shape
    nt = seq // bq
    sub = bq // WINDOW
    nblk = seq // WINDOW
    n_mem = km.shape[1]
    tile = lambda bi, i: (bi, i, 0)
    prev = lambda bi, i: (bi, jnp.maximum(i * sub - 1, 0), 0)
    nxt = lambda bi, i: (bi, jnp.minimum((i + 1) * sub, nblk - 1), 0)
    per_b = lambda bi, i: (bi, 0, 0)
    fixed = lambda bi, i: (0, 0)
    kv_specs = [pl.BlockSpec((None, WINDOW, A_KV), prev), pl.BlockSpec((None, bq, A_KV), tile),
                pl.BlockSpec((None, WINDOW, A_KV), nxt)]
    return pl.pallas_call(
        _local_kernel,
        grid=(b, nt),
        in_specs=[pl.BlockSpec(memory_space=pltpu.SMEM),
                  pl.BlockSpec((None, bq, A_Q), tile)] + kv_specs + kv_specs + [
                  pl.BlockSpec((None, bq, M_Q), tile),
                  pl.BlockSpec((None, n_mem, M_Q), per_b), pl.BlockSpec((None, n_mem, M_Q), per_b),
                  _resident((1, A_Q), fixed), _resident((1, M_Q), fixed)],
        out_specs=[pl.BlockSpec((None, bq, A_Q), tile), pl.BlockSpec((None, bq, M_Q), tile)],
        out_shape=[jax.ShapeDtypeStruct((b, seq, A_Q), BF16), jax.ShapeDtypeStruct((b, seq, M_Q), BF16)],
        scratch_shapes=[pltpu.VMEM((bq + 2 * WINDOW, A_KV), BF16), pltpu.VMEM((bq + 2 * WINDOW, A_KV), BF16),
                        pltpu.VMEM((bq, A_Q), F32), pltpu.VMEM((bq, M_Q), F32)],
        compiler_params=_params("parallel", "parallel"),
        name="local_attn",
    )(sink, qa, ka, ka, ka, va, va, va, qm, km, vm, ga.reshape(1, A_Q), gm.reshape(1, M_Q))


def _global_kernel(q_ref, k_ref, v_ref, g_ref, o_ref, obuf, *, bk):
    bq = q_ref.shape[0]
    seq = k_ref.shape[0]
    group = B_HEADS // B_KV_HEADS
    rows = group * bq
    nchunk = bk // HEAD_DIM

    for g in range(B_KV_HEADS):
        q = jnp.concatenate(
            [q_ref[:, (g * group + h) * HEAD_DIM:(g * group + h + 1) * HEAD_DIM] for h in range(group)], axis=0)
        ksl = slice(g * HEAD_DIM, (g + 1) * HEAD_DIM)

        def body(j, carry, q=q, ksl=ksl):
            m, l, acc = carry
            start = pl.multiple_of(j * bk, bk)
            s = _dot_nt(q, k_ref[pl.ds(start, bk), ksl])
            chunks = [s[:, c * HEAD_DIM:(c + 1) * HEAD_DIM] for c in range(nchunk)]
            m_new = jnp.maximum(m, jnp.max(functools.reduce(jnp.maximum, chunks), axis=-1, keepdims=True))
            alpha = jnp.exp(m - m_new)
            ps = [jnp.exp(c - m_new) for c in chunks]
            l = alpha * l + functools.reduce(jnp.add, ps)
            p = jnp.concatenate(ps, axis=1).astype(BF16)
            acc = alpha * acc + _dot(p, v_ref[pl.ds(start, bk), ksl])
            return m_new, l, acc

        init = (jnp.full((rows, HEAD_DIM), NEG_INF, F32), jnp.zeros((rows, HEAD_DIM), F32),
                jnp.zeros((rows, HEAD_DIM), F32))
        _, l, acc = lax.fori_loop(0, seq // bk, body, init)
        o = acc * (1.0 / jnp.sum(l, axis=-1, keepdims=True))
        for h in range(group):
            c0 = (g * group + h) * HEAD_DIM
            obuf[:, c0:c0 + HEAD_DIM] = o[h * bq:(h + 1) * bq]
    o_ref[...] = _rms(obuf[...], g_ref[...]).astype(o_ref.dtype)


def _global_attention(q, k, v, gb, bq, bk):
    b, seq, _ = q.shape
    tile = lambda bi, i: (bi, i, 0)
    per_b = lambda bi, i: (bi, 0, 0)
    fixed = lambda bi, i: (0, 0)
    return pl.pallas_call(
        functools.partial(_global_kernel, bk=bk),
        grid=(b, seq // bq),
        in_specs=[pl.BlockSpec((None, bq, B_Q), tile),
                  _resident((None, seq, B_KV), per_b), _resident((None, seq, B_KV), per_b),
                  _resident((1, B_Q), fixed)],
        out_specs=pl.BlockSpec((None, bq, B_Q), tile),
        out_shape=jax.ShapeDtypeStruct((b, seq, B_Q), BF16),
        scratch_shapes=[pltpu.VMEM((bq, B_Q), F32)],
        compiler_params=_params("parallel", "parallel"),
        name="global_attn",
    )(q, k, v, gb.reshape(1, B_Q))


def _outproj_kernel(oa_ref, ob_ref, om_ref, x_ref, w_ref, x1_ref):
    merged = jnp.concatenate([oa_ref[...], ob_ref[...], om_ref[...]], axis=1)
    x1_ref[...] = x_ref[...] + _dot(merged, w_ref[...])


def _outproj(oa, ob, om, x2, w_bf, tm):
    tokens, d = x2.shape
    row = lambda i: (i, 0)
    fixed = lambda i: (0, 0)
    return pl.pallas_call(
        _outproj_kernel,
        grid=(tokens // tm,),
        in_specs=[pl.BlockSpec((tm, A_Q), row), pl.BlockSpec((tm, B_Q), row), pl.BlockSpec((tm, M_Q), row),
                  pl.BlockSpec((tm, d), row), _resident(w_bf.shape, fixed)],
        out_specs=pl.BlockSpec((tm, d), row),
        out_shape=jax.ShapeDtypeStruct((tokens, d), F32),
        compiler_params=_params("parallel"),
        name="outproj",
    )(oa, ob, om, x2, w_bf)


def _ffn_kernel(x1_ref, gf_ref, wg_ref, wu_ref, wd_ref, gl_ref, o_ref, h_scr):
    j = pl.program_id(1)

    @pl.when(j == 0)
    def _():
        x1 = x1_ref[...]
        h_scr[...] = _rms(x1, gf_ref[...]).astype(h_scr.dtype)
        o_ref[...] = x1

    h = h_scr[...]
    gate = _dot(h, wg_ref[...])
    up = _dot(h, wu_ref[...])
    act = (gate * jax.nn.sigmoid(gate) * up).astype(BF16)
    o_ref[...] += _dot(act, wd_ref[...])

    @pl.when(j == pl.num_programs(1) - 1)
    def _():
        o_ref[...] = _rms(o_ref[...], gl_ref[...])


def _ffn(x1, gf, w_gu_bf, w_dn_bf, gl, tm, tf):
    tokens, d = x1.shape
    d_ff = w_dn_bf.shape[0]
    nf = d_ff // tf
    row = lambda i, j: (i, 0)
    fixed = lambda i, j: (0, 0)
    return pl.pallas_call(
        _ffn_kernel,
        grid=(tokens // tm, nf),
        in_specs=[pl.BlockSpec((tm, d), row), _resident((1, d), fixed),
                  pl.BlockSpec((d, tf), lambda i, j: (0, j)), pl.BlockSpec((d, tf), lambda i, j: (0, j + nf)),
                  pl.BlockSpec((tf, d), lambda i, j: (j, 0)), _resident((1, d), fixed)],
        out_specs=pl.BlockSpec((tm, d), row),
        out_shape=jax.ShapeDtypeStruct((tokens, d), F32),
        scratch_shapes=[pltpu.VMEM((tm, d), BF16)],
        compiler_params=_params("parallel", "arbitrary"),
        name="ffn",
    )(x1, gf.reshape(1, d), w_gu_bf, w_gu_bf, w_dn_bf, gl.reshape(1, d))


def _tile(n, pref):
    t = min(pref, n)
    assert n % t == 0, (n, t)
    return t


def _layer(x, mem, norm_mix_g, norm_mem_g, w_in, w_mem_kv, sink_a, q_norm_b_g, k_norm_b_g,
           out_norm_g, w_out, norm_ffn_g, w_gate_up, w_down, final_g):
    b, seq, d = x.shape
    tokens = b * seq
    assert seq % GRID_W == 0 and seq % WINDOW == 0
    x2 = x.reshape(tokens, d)
    tm = _tile(seq, 512)

    qa, ka, va, qb, kb, vb, qm = _inproj(x2, seq, norm_mix_g, w_in.astype(BF16), q_norm_b_g, k_norm_b_g, tm)
    km, vm = _memkv(mem, norm_mem_g, w_mem_kv.astype(BF16))

    r3 = lambda t: t.reshape(b, seq, t.shape[-1])
    oa, om = _local_attention(r3(qa), r3(ka), r3(va), r3(qm), km, vm, sink_a,
                              out_norm_g[:A_Q], out_norm_g[A_Q + B_Q:], _tile(seq, 512))
    ob = _global_attention(r3(qb), r3(kb), r3(vb), out_norm_g[A_Q:A_Q + B_Q], _tile(seq, 256), _tile(seq, 512))

    x1 = _outproj(oa.reshape(tokens, A_Q), ob.reshape(tokens, B_Q), om.reshape(tokens, M_Q), x2,
                  w_out.astype(BF16), tm)
    d_ff = w_down.shape[0]
    y = _ffn(x1, norm_ffn_g, w_gate_up.astype(BF16), w_down.astype(BF16), final_g, tm, _tile(d_ff, 512))
    return y.reshape(b, seq, d)


def kernel(x_prompt, x_sample, mem_prompt, mem_sample, norm_mix_g, norm_mem_g, w_in, w_mem_kv, sink_a,
           q_norm_b_g, k_norm_b_g, out_norm_g, w_out, norm_ffn_g, w_gate_up, w_down, norm_final_g):
    depth = w_in.shape[0]
    assert depth == 1, "final norm is fused into the last layer's ffn kernel"
    args = (norm_mix_g[0], norm_mem_g[0], w_in[0], w_mem_kv[0], sink_a[0], q_norm_b_g[0], k_norm_b_g[0],
            out_norm_g[0], w_out[0], norm_ffn_g[0], w_gate_up[0], w_down[0], norm_final_g)
    return (_layer(x_prompt, mem_prompt, *args), _layer(x_sample, mem_sample, *args))
```

```python
import functools

import jax
import jax.numpy as jnp
import numpy as np
from jax import lax
from jax.experimental import pallas as pl
from jax.experimental.pallas import tpu as pltpu

HEAD_DIM = 128
A_HEADS, A_KV_HEADS = 8, 2
B_HEADS, B_KV_HEADS = 4, 2
M_HEADS = 4
WINDOW = 128
GRID_W = 64
ROPE_THETA = 10000.0
NORM_EPS = 1e-6
NEG_INF = -1e30
A_Q, A_KV = A_HEADS * HEAD_DIM, A_KV_HEADS * HEAD_DIM
B_Q, B_KV = B_HEADS * HEAD_DIM, B_KV_HEADS * HEAD_DIM
M_Q = M_HEADS * HEAD_DIM
LOG2E = 1.4426950408889634
Q_SCALE = HEAD_DIM ** -0.5 * LOG2E
SOFTMAX_BOUND_LIMIT = 32.0
ROUNDING_SLACK = 1.02

BF16 = jnp.bfloat16
F32 = jnp.float32

VMEM_LIMIT_BYTES = 56 * 1024 * 1024


def _params(*sem):
    return pltpu.CompilerParams(dimension_semantics=sem, vmem_limit_bytes=VMEM_LIMIT_BYTES)


def _resident(shape, index_map):
    return pl.BlockSpec(shape, index_map, pipeline_mode=pl.Buffered(1))


def _rms(x, g):
    ms = jnp.mean(x * x, axis=-1, keepdims=True)
    return x * lax.rsqrt(ms + NORM_EPS) * g


def _dot(a, b):
    return jnp.dot(a, b, preferred_element_type=F32)


def _dot_nt(a, b):
    return lax.dot_general(a, b, (((1,), (1,)), ((), ())), preferred_element_type=F32)


def _inproj_kernel(x_ref, g_ref, w_ref, ca_ref, sa_ref, cb_ref, sb_ref, qg_ref, kg_ref,
                   qa_ref, ka_ref, va_ref, qb_ref, kb_ref, vb_ref, qm_ref):
    h = _rms(x_ref[...], g_ref[...]).astype(BF16)
    ca, sa, cb, sb = ca_ref[...], sa_ref[...], cb_ref[...], sb_ref[...]
    lane = lax.broadcasted_iota(jnp.int32, ca.shape, 1)
    low_quarter = (lane % (HEAD_DIM // 2)) < (HEAD_DIM // 4)

    def rope_a(t):
        return t * ca + pltpu.roll(t, HEAD_DIM // 2, 1) * sa

    def rope_b(t):
        partner = jnp.where(low_quarter, pltpu.roll(t, HEAD_DIM - HEAD_DIM // 4, 1),
                            pltpu.roll(t, HEAD_DIM // 4, 1))
        return t * cb + partner * sb

    def heads(t, n, fn, out_ref):
        for i in range(n):
            sl = slice(i * HEAD_DIM, (i + 1) * HEAD_DIM)
            out_ref[:, sl] = fn(t[:, sl]).astype(out_ref.dtype)

    off = 0

    def section(width):
        nonlocal off
        t = _dot(h, w_ref[:, off:off + width])
        off += width
        return t

    heads(section(A_Q), A_HEADS, lambda t: rope_a(t) * Q_SCALE, qa_ref)
    heads(section(A_KV), A_KV_HEADS, rope_a, ka_ref)
    va_ref[...] = section(A_KV).astype(va_ref.dtype)
    qg, kg = qg_ref[...], kg_ref[...]
    heads(section(B_Q), B_HEADS, lambda t: rope_b(_rms(t, qg)) * Q_SCALE, qb_ref)
    heads(section(B_KV), B_KV_HEADS, lambda t: rope_b(_rms(t, kg)), kb_ref)
    vb_ref[...] = section(B_KV).astype(vb_ref.dtype)
    qm_ref[...] = (section(M_Q) * Q_SCALE).astype(qm_ref.dtype)


def _rope_tables(seq):
    def tables(pos, dim):
        inv = ROPE_THETA ** (-(jnp.arange(0, dim, 2, dtype=F32) / dim))
        ang = pos[:, None] * inv[None, :]
        return jnp.cos(ang), jnp.sin(ang)

    t = jnp.arange(seq, dtype=jnp.int32)
    c, s = tables(t.astype(F32), HEAD_DIM)
    ca = jnp.concatenate([c, c], axis=-1)
    sa = jnp.concatenate([-s, s], axis=-1)
    rows = (t // GRID_W).astype(F32)
    cols = (t % GRID_W).astype(F32)
    cr, sr = tables(rows, HEAD_DIM // 2)
    cc, sc = tables(cols, HEAD_DIM // 2)
    cb = jnp.concatenate([cr, cr, cc, cc], axis=-1)
    sb = jnp.concatenate([-sr, sr, -sc, sc], axis=-1)
    return ca, sa, cb, sb


def _inproj(x2, seq, g, w_bf, qg, kg, tm):
    tokens, d = x2.shape
    per_seq = seq // tm
    ca, sa, cb, sb = _rope_tables(seq)
    row = lambda i: (i, 0)
    fixed = lambda i: (0, 0)
    tab = lambda i: (i % per_seq, 0)
    widths = (A_Q, A_KV, A_KV, B_Q, B_KV, B_KV, M_Q)
    return pl.pallas_call(
        _inproj_kernel,
        grid=(tokens // tm,),
        in_specs=[pl.BlockSpec((tm, d), row),
                  _resident((1, d), fixed),
                  _resident(w_bf.shape, fixed),
                  pl.BlockSpec((tm, HEAD_DIM), tab), pl.BlockSpec((tm, HEAD_DIM), tab),
                  pl.BlockSpec((tm, HEAD_DIM), tab), pl.BlockSpec((tm, HEAD_DIM), tab),
                  _resident((1, HEAD_DIM), fixed), _resident((1, HEAD_DIM), fixed)],
        out_specs=[pl.BlockSpec((tm, w), row) for w in widths],
        out_shape=[jax.ShapeDtypeStruct((tokens, w), BF16) for w in widths],
        compiler_params=_params("parallel"),
        name="inproj",
    )(x2, g.reshape(1, d), w_bf, ca, sa, cb, sb, qg.reshape(1, HEAD_DIM), kg.reshape(1, HEAD_DIM))


def _memkv_kernel(mem_ref, g_ref, w_ref, km_ref, vm_ref):
    h = _rms(mem_ref[...], g_ref[...]).astype(BF16)
    kv = _dot(h, w_ref[...])
    km_ref[...] = kv[:, :M_Q].astype(km_ref.dtype)
    vm_ref[...] = kv[:, M_Q:].astype(vm_ref.dtype)


def _memkv(mem, g, w_bf):
    b, n, d = mem.shape
    blk = lambda i: (i, 0, 0)
    fixed = lambda i: (0, 0)
    return pl.pallas_call(
        _memkv_kernel,
        grid=(b,),
        in_specs=[pl.BlockSpec((None, n, d), blk), _resident((1, d), fixed), _resident(w_bf.shape, fixed)],
        out_specs=[pl.BlockSpec((None, n, M_Q), blk), pl.BlockSpec((None, n, M_Q), blk)],
        out_shape=[jax.ShapeDtypeStruct((b, n, M_Q), BF16)] * 2,
        compiler_params=_params("parallel"),
        name="memkv",
    )(mem, g.reshape(1, d), w_bf)


def _local_kernel(sink_ref, qa_ref, kp_ref, kc_ref, kn_ref, vp_ref, vc_ref, vn_ref,
                  qm_ref, km_ref, vm_ref, ga_ref, gm_ref, oa_ref, om_ref, kbuf, vbuf, obuf, mbuf):
    bq = qa_ref.shape[0]
    nsub = bq // WINDOW
    i = pl.program_id(1)
    first = i == 0
    last = i == pl.num_programs(1) - 1

    kbuf[0:WINDOW] = kp_ref[...]
    kbuf[WINDOW:WINDOW + bq] = kc_ref[...]
    kbuf[WINDOW + bq:] = kn_ref[...]
    vbuf[0:WINDOW] = vp_ref[...]
    vbuf[WINDOW:WINDOW + bq] = vc_ref[...]
    vbuf[WINDOW + bq:] = vn_ref[...]

    group = A_HEADS // A_KV_HEADS
    rows, cols = group * WINDOW, 3 * WINDOW
    qpos = lax.broadcasted_iota(jnp.int32, (rows, cols), 0) % WINDOW
    kcol = lax.broadcasted_iota(jnp.int32, (rows, cols), 1)
    band = jnp.abs(kcol - WINDOW - qpos) <= WINDOW
    lo = jnp.where(first, WINDOW, 0)
    hi = jnp.where(last, 2 * WINDOW, 3 * WINDOW)

    for j in range(nsub):
        valid = band
        if j == 0:
            valid = valid & (kcol >= lo)
        if j == nsub - 1:
            valid = valid & (kcol < hi)
        for g in range(A_KV_HEADS):
            q = jnp.concatenate(
                [qa_ref[j * WINDOW:(j + 1) * WINDOW, (g * group + h) * HEAD_DIM:(g * group + h + 1) * HEAD_DIM]
                 for h in range(group)], axis=0)
            k = kbuf[j * WINDOW:j * WINDOW + cols, g * HEAD_DIM:(g + 1) * HEAD_DIM]
            v = vbuf[j * WINDOW:j * WINDOW + cols, g * HEAD_DIM:(g + 1) * HEAD_DIM]
            s = jnp.where(valid, _dot_nt(q, k), NEG_INF)
            sink = jnp.concatenate(
                [jnp.full((WINDOW, 1), sink_ref[g * group + h] * LOG2E, F32) for h in range(group)], axis=0)
            m = jnp.maximum(jnp.max(s, axis=-1, keepdims=True), sink)
            p = jnp.exp2(s - m)
            denom = jnp.sum(p, axis=-1, keepdims=True) + jnp.exp2(sink - m)
            o = _dot(p.astype(BF16), v) * (1.0 / denom)
            for h in range(group):
                c0 = (g * group + h) * HEAD_DIM
                obuf[j * WINDOW:(j + 1) * WINDOW, c0:c0 + HEAD_DIM] = o[h * WINDOW:(h + 1) * WINDOW]
    oa_ref[...] = _rms(obuf[...], ga_ref[...]).astype(oa_ref.dtype)

    for h in range(M_HEADS):
        sl = slice(h * HEAD_DIM, (h + 1) * HEAD_DIM)
        s = _dot_nt(qm_ref[:, sl], km_ref[:, sl])
        m = jnp.max(s, axis=-1, keepdims=True)
        p = jnp.exp2(s - m)
        denom = jnp.sum(p, axis=-1, keepdims=True)
        mbuf[:, sl] = _dot(p.astype(BF16), vm_ref[:, sl]) * (1.0 / denom)
    om_ref[...] = _rms(mbuf[...], gm_ref[...]).astype(om_ref.dtype)


def _local_attention(qa, ka, va, qm, km, vm, sink, ga, gm, bq):
    b, seq, _ = qa.shape
    nt = seq // bq
    sub = bq // WINDOW
    nblk = seq // WINDOW
    n_mem = km.shape[1]
    tile = lambda bi, i: (bi, i, 0)
    prev = lambda bi, i: (bi, jnp.maximum(i * sub - 1, 0), 0)
    nxt = lambda bi, i: (bi, jnp.minimum((i + 1) * sub, nblk - 1), 0)
    per_b = lambda bi, i: (bi, 0, 0)
    fixed = lambda bi, i: (0, 0)
    kv_specs = [pl.BlockSpec((None, WINDOW, A_KV), prev), pl.BlockSpec((None, bq, A_KV), tile),
                pl.BlockSpec((None, WINDOW, A_KV), nxt)]
    return pl.pallas_call(
        _local_kernel,
        grid=(b, nt),
        in_specs=[pl.BlockSpec(memory_space=pltpu.SMEM),
                  pl.BlockSpec((None, bq, A_Q), tile)] + kv_specs + kv_specs + [
                  pl.BlockSpec((None, bq, M_Q), tile),
                  pl.BlockSpec((None, n_mem, M_Q), per_b), pl.BlockSpec((None, n_mem, M_Q), per_b),
                  _resident((1, A_Q), fixed), _resident((1, M_Q), fixed)],
        out_specs=[pl.BlockSpec((None, bq, A_Q), tile), pl.BlockSpec((None, bq, M_Q), tile)],
        out_shape=[jax.ShapeDtypeStruct((b, seq, A_Q), BF16), jax.ShapeDtypeStruct((b, seq, M_Q), BF16)],
        scratch_shapes=[pltpu.VMEM((bq + 2 * WINDOW, A_KV), BF16), pltpu.VMEM((bq + 2 * WINDOW, A_KV), BF16),
                        pltpu.VMEM((bq, A_Q), F32), pltpu.VMEM((bq, M_Q), F32)],
        compiler_params=_params("parallel", "parallel"),
        name="local_attn",
    )(sink, qa, ka, ka, ka, va, va, va, qm, km, vm, ga.reshape(1, A_Q), gm.reshape(1, M_Q))


def _global_heads(q_ref, k_ref, v_ref, obuf, bk, bounded):
    bq = q_ref.shape[0]
    seq = k_ref.shape[0]
    group = B_HEADS // B_KV_HEADS
    rows = group * bq
    nchunk = bk // HEAD_DIM
    zeros = jnp.zeros((rows, HEAD_DIM), F32)

    for g in range(B_KV_HEADS):
        q = jnp.concatenate(
            [q_ref[:, (g * group + h) * HEAD_DIM:(g * group + h + 1) * HEAD_DIM] for h in range(group)], axis=0)
        ksl = slice(g * HEAD_DIM, (g + 1) * HEAD_DIM)

        def logits(j, q=q, ksl=ksl):
            start = pl.multiple_of(j * bk, bk)
            s = _dot_nt(q, k_ref[pl.ds(start, bk), ksl])
            return [s[:, c * HEAD_DIM:(c + 1) * HEAD_DIM] for c in range(nchunk)], v_ref[pl.ds(start, bk), ksl]

        def bounded_body(j, carry, logits=logits):
            l, acc = carry
            chunks, v = logits(j)
            ps = [jnp.exp2(c) for c in chunks]
            l = l + functools.reduce(jnp.add, ps)
            acc = acc + _dot(jnp.concatenate(ps, axis=1).astype(BF16), v)
            return l, acc

        def online_body(j, carry, logits=logits):
            m, l, acc = carry
            chunks, v = logits(j)
            m_new = jnp.maximum(m, jnp.max(functools.reduce(jnp.maximum, chunks), axis=-1, keepdims=True))
            alpha = jnp.exp2(m - m_new)
            ps = [jnp.exp2(c - m_new) for c in chunks]
            l = alpha * l + functools.reduce(jnp.add, ps)
            acc = alpha * acc + _dot(jnp.concatenate(ps, axis=1).astype(BF16), v)
            return m_new, l, acc

        if bounded:
            l, acc = lax.fori_loop(0, seq // bk, bounded_body, (zeros, zeros))
        else:
            _, l, acc = lax.fori_loop(0, seq // bk, online_body,
                                      (jnp.full((rows, HEAD_DIM), NEG_INF, F32), zeros, zeros))
        o = acc * (1.0 / jnp.sum(l, axis=-1, keepdims=True))
        for h in range(group):
            c0 = (g * group + h) * HEAD_DIM
            obuf[:, c0:c0 + HEAD_DIM] = o[h * bq:(h + 1) * bq]


def _global_kernel(bounded_ref, q_ref, k_ref, v_ref, g_ref, o_ref, obuf, *, bk):
    bounded = bounded_ref[0] == 1

    @pl.when(bounded)
    def _():
        _global_heads(q_ref, k_ref, v_ref, obuf, bk, True)

    @pl.when(jnp.logical_not(bounded))
    def _():
        _global_heads(q_ref, k_ref, v_ref, obuf, bk, False)

    o_ref[...] = _rms(obuf[...], g_ref[...]).astype(o_ref.dtype)


def _logits_bounded(qg, kg):
    bound = (HEAD_DIM ** 0.5) * jnp.max(jnp.abs(qg)) * jnp.max(jnp.abs(kg)) * ROUNDING_SLACK
    return (bound <= SOFTMAX_BOUND_LIMIT).astype(jnp.int32).reshape(1)


def _global_attention(q, k, v, gb, bounded, bq, bk):
    b, seq, _ = q.shape
    tile = lambda bi, i: (bi, i, 0)
    per_b = lambda bi, i: (bi, 0, 0)
    fixed = lambda bi, i: (0, 0)
    return pl.pallas_call(
        functools.partial(_global_kernel, bk=bk),
        grid=(b, seq // bq),
        in_specs=[pl.BlockSpec(memory_space=pltpu.SMEM),
                  pl.BlockSpec((None, bq, B_Q), tile),
                  _resident((None, seq, B_KV), per_b), _resident((None, seq, B_KV), per_b),
                  _resident((1, B_Q), fixed)],
        out_specs=pl.BlockSpec((None, bq, B_Q), tile),
        out_shape=jax.ShapeDtypeStruct((b, seq, B_Q), BF16),
        scratch_shapes=[pltpu.VMEM((bq, B_Q), F32)],
        compiler_params=_params("parallel", "parallel"),
        name="global_attn",
    )(bounded, q, k, v, gb.reshape(1, B_Q))


def _outproj_kernel(oa_ref, ob_ref, om_ref, x_ref, w_ref, x1_ref):
    merged = jnp.concatenate([oa_ref[...], ob_ref[...], om_ref[...]], axis=1)
    x1_ref[...] = x_ref[...] + _dot(merged, w_ref[...])


def _outproj(oa, ob, om, x2, w_bf, tm):
    tokens, d = x2.shape
    row = lambda i: (i, 0)
    fixed = lambda i: (0, 0)
    return pl.pallas_call(
        _outproj_kernel,
        grid=(tokens // tm,),
        in_specs=[pl.BlockSpec((tm, A_Q), row), pl.BlockSpec((tm, B_Q), row), pl.BlockSpec((tm, M_Q), row),
                  pl.BlockSpec((tm, d), row), _resident(w_bf.shape, fixed)],
        out_specs=pl.BlockSpec((tm, d), row),
        out_shape=jax.ShapeDtypeStruct((tokens, d), F32),
        compiler_params=_params("parallel"),
        name="outproj",
    )(oa, ob, om, x2, w_bf)


def _ffn_kernel(x1_ref, gf_ref, wg_ref, wu_ref, wd_ref, gl_ref, o_ref, h_scr):
    j = pl.program_id(1)

    @pl.when(j == 0)
    def _():
        x1 = x1_ref[...]
        h_scr[...] = _rms(x1, gf_ref[...]).astype(h_scr.dtype)
        o_ref[...] = x1

    h = h_scr[...]
    gate = _dot(h, wg_ref[...])
    up = _dot(h, wu_ref[...])
    act = (gate * jax.nn.sigmoid(gate) * up).astype(BF16)
    o_ref[...] += _dot(act, wd_ref[...])

    @pl.when(j == pl.num_programs(1) - 1)
    def _():
        o_ref[...] = _rms(o_ref[...], gl_ref[...])


def _ffn(x1, gf, w_gu_bf, w_dn_bf, gl, tm, tf):
    tokens, d = x1.shape
    d_ff = w_dn_bf.shape[0]
    nf = d_ff // tf
    row = lambda i, j: (i, 0)
    fixed = lambda i, j: (0, 0)
    return pl.pallas_call(
        _ffn_kernel,
        grid=(tokens // tm, nf),
        in_specs=[pl.BlockSpec((tm, d), row), _resident((1, d), fixed),
                  pl.BlockSpec((d, tf), lambda i, j: (0, j)), pl.BlockSpec((d, tf), lambda i, j: (0, j + nf)),
                  pl.BlockSpec((tf, d), lambda i, j: (j, 0)), _resident((1, d), fixed)],
        out_specs=pl.BlockSpec((tm, d), row),
        out_shape=jax.ShapeDtypeStruct((tokens, d), F32),
        scratch_shapes=[pltpu.VMEM((tm, d), BF16)],
        compiler_params=_params("parallel", "arbitrary"),
        name="ffn",
    )(x1, gf.reshape(1, d), w_gu_bf, w_gu_bf, w_dn_bf, gl.reshape(1, d))


def _tile(n, pref):
    t = min(pref, n)
    assert n % t == 0, (n, t)
    return t


def _layer(x, mem, norm_mix_g, norm_mem_g, w_in, w_mem_kv, sink_a, q_norm_b_g, k_norm_b_g,
           out_norm_g, w_out, norm_ffn_g, w_gate_up, w_down, final_g):
    b, seq, d = x.shape
    tokens = b * seq
    assert seq % GRID_W == 0 and seq % WINDOW == 0
    x2 = x.reshape(tokens, d)
    tm = _tile(seq, 512)

    qa, ka, va, qb, kb, vb, qm = _inproj(x2, seq, norm_mix_g, w_in.astype(BF16), q_norm_b_g, k_norm_b_g, tm)
    km, vm = _memkv(mem, norm_mem_g, w_mem_kv.astype(BF16))

    r3 = lambda t: t.reshape(b, seq, t.shape[-1])
    oa, om = _local_attention(r3(qa), r3(ka), r3(va), r3(qm), km, vm, sink_a,
                              out_norm_g[:A_Q], out_norm_g[A_Q + B_Q:], _tile(seq, 512))
    ob = _global_attention(r3(qb), r3(kb), r3(vb), out_norm_g[A_Q:A_Q + B_Q],
                           _logits_bounded(q_norm_b_g, k_norm_b_g), _tile(seq, 512), _tile(seq, 2048))

    x1 = _outproj(oa.reshape(tokens, A_Q), ob.reshape(tokens, B_Q), om.reshape(tokens, M_Q), x2,
                  w_out.astype(BF16), tm)
    d_ff = w_down.shape[0]
    y = _ffn(x1, norm_ffn_g, w_gate_up.astype(BF16), w_down.astype(BF16), final_g, tm, _tile(d_ff, 512))
    return y.reshape(b, seq, d)


def kernel(x_prompt, x_sample, mem_prompt, mem_sample, norm_mix_g, norm_mem_g, w_in, w_mem_kv, sink_a,
           q_norm_b_g, k_norm_b_g, out_norm_g, w_out, norm_ffn_g, w_gate_up, w_down, norm_final_g):
    depth = w_in.shape[0]
    assert depth == 1, "final norm is fused into the last layer's ffn kernel"
    args = (norm_mix_g[0], norm_mem_g[0], w_in[0], w_mem_kv[0], sink_a[0], q_norm_b_g[0], k_norm_b_g[0],
            out_norm_g[0], w_out[0], norm_ffn_g[0], w_gate_up[0], w_down[0], norm_final_g)
    return (_layer(x_prompt, mem_prompt, *args), _layer(x_sample, mem_sample, *args))
```

```python
import functools

import jax
import jax.numpy as jnp
import numpy as np
from jax import lax
from jax.experimental import pallas as pl
from jax.experimental.pallas import tpu as pltpu

HEAD_DIM = 128
A_HEADS, A_KV_HEADS = 8, 2
B_HEADS, B_KV_HEADS = 4, 2
M_HEADS = 4
WINDOW = 128
GRID_W = 64
ROPE_THETA = 10000.0
NORM_EPS = 1e-6
NEG_INF = -1e30
A_Q, A_KV = A_HEADS * HEAD_DIM, A_KV_HEADS * HEAD_DIM
B_Q, B_KV = B_HEADS * HEAD_DIM, B_KV_HEADS * HEAD_DIM
M_Q = M_HEADS * HEAD_DIM
LOG2E = 1.4426950408889634
Q_SCALE = HEAD_DIM ** -0.5 * LOG2E
SOFTMAX_BOUND_LIMIT = 32.0
ROUNDING_SLACK = 1.02

BF16 = jnp.bfloat16
F32 = jnp.float32

VMEM_LIMIT_BYTES = 56 * 1024 * 1024


def _params(*sem):
    return pltpu.CompilerParams(dimension_semantics=sem, vmem_limit_bytes=VMEM_LIMIT_BYTES)


def _resident(shape, index_map):
    return pl.BlockSpec(shape, index_map, pipeline_mode=pl.Buffered(1))


def _rms(x, g):
    ms = jnp.mean(x * x, axis=-1, keepdims=True)
    return x * lax.rsqrt(ms + NORM_EPS) * g


def _dot(a, b):
    return jnp.dot(a, b, preferred_element_type=F32)


def _dot_nt(a, b):
    return lax.dot_general(a, b, (((1,), (1,)), ((), ())), preferred_element_type=F32)


def _inproj_kernel(x_ref, g_ref, w_ref, ca_ref, sa_ref, cb_ref, sb_ref, qg_ref, kg_ref,
                   qa_ref, ka_ref, va_ref, qb_ref, kb_ref, vb_ref, qm_ref):
    h = _rms(x_ref[...], g_ref[...]).astype(BF16)
    ca, sa, cb, sb = ca_ref[...], sa_ref[...], cb_ref[...], sb_ref[...]
    lane = lax.broadcasted_iota(jnp.int32, ca.shape, 1)
    low_quarter = (lane % (HEAD_DIM // 2)) < (HEAD_DIM // 4)

    def rope_a(t):
        return t * ca + pltpu.roll(t, HEAD_DIM // 2, 1) * sa

    def rope_b(t):
        partner = jnp.where(low_quarter, pltpu.roll(t, HEAD_DIM - HEAD_DIM // 4, 1),
                            pltpu.roll(t, HEAD_DIM // 4, 1))
        return t * cb + partner * sb

    def heads(t, n, fn, out_ref):
        for i in range(n):
            sl = slice(i * HEAD_DIM, (i + 1) * HEAD_DIM)
            out_ref[:, sl] = fn(t[:, sl]).astype(out_ref.dtype)

    off = 0

    def section(width):
        nonlocal off
        t = _dot(h, w_ref[:, off:off + width])
        off += width
        return t

    heads(section(A_Q), A_HEADS, lambda t: rope_a(t) * Q_SCALE, qa_ref)
    heads(section(A_KV), A_KV_HEADS, rope_a, ka_ref)
    va_ref[...] = section(A_KV).astype(va_ref.dtype)
    qg, kg = qg_ref[...], kg_ref[...]
    heads(section(B_Q), B_HEADS, lambda t: rope_b(_rms(t, qg)) * Q_SCALE, qb_ref)
    heads(section(B_KV), B_KV_HEADS, lambda t: rope_b(_rms(t, kg)), kb_ref)
    vb_ref[...] = section(B_KV).astype(vb_ref.dtype)
    qm_ref[...] = (section(M_Q) * Q_SCALE).astype(qm_ref.dtype)


def _rope_tables(seq):
    def tables(pos, dim):
        inv = ROPE_THETA ** (-(jnp.arange(0, dim, 2, dtype=F32) / dim))
        ang = pos[:, None] * inv[None, :]
        return jnp.cos(ang), jnp.sin(ang)

    c, s = tables(jnp.arange(seq, dtype=F32), HEAD_DIM)
    ca = jnp.concatenate([c, c], axis=-1)
    sa = jnp.concatenate([-s, s], axis=-1)
    n_rows = seq // GRID_W
    quarter = HEAD_DIM // 4
    per_row = lambda t: jnp.broadcast_to(t[:, None, :], (n_rows, GRID_W, quarter)).reshape(seq, quarter)
    per_col = lambda t: jnp.broadcast_to(t[None, :, :], (n_rows, GRID_W, quarter)).reshape(seq, quarter)
    cr, sr = (per_row(t) for t in tables(jnp.arange(n_rows, dtype=F32), HEAD_DIM // 2))
    cc, sc = (per_col(t) for t in tables(jnp.arange(GRID_W, dtype=F32), HEAD_DIM // 2))
    cb = jnp.concatenate([cr, cr, cc, cc], axis=-1)
    sb = jnp.concatenate([-sr, sr, -sc, sc], axis=-1)
    return ca, sa, cb, sb


def _inproj(x2, seq, g, w_bf, qg, kg, rope, tm):
    tokens, d = x2.shape
    per_seq = seq // tm
    ca, sa, cb, sb = rope
    row = lambda i: (i, 0)
    fixed = lambda i: (0, 0)
    tab = lambda i: (i % per_seq, 0)
    widths = (A_Q, A_KV, A_KV, B_Q, B_KV, B_KV, M_Q)
    return pl.pallas_call(
        _inproj_kernel,
        grid=(tokens // tm,),
        in_specs=[pl.BlockSpec((tm, d), row),
                  _resident((1, d), fixed),
                  _resident(w_bf.shape, fixed),
                  pl.BlockSpec((tm, HEAD_DIM), tab), pl.BlockSpec((tm, HEAD_DIM), tab),
                  pl.BlockSpec((tm, HEAD_DIM), tab), pl.BlockSpec((tm, HEAD_DIM), tab),
                  _resident((1, HEAD_DIM), fixed), _resident((1, HEAD_DIM), fixed)],
        out_specs=[pl.BlockSpec((tm, w), row) for w in widths],
        out_shape=[jax.ShapeDtypeStruct((tokens, w), BF16) for w in widths],
        compiler_params=_params("parallel"),
        name="inproj",
    )(x2, g.reshape(1, d), w_bf, ca, sa, cb, sb, qg.reshape(1, HEAD_DIM), kg.reshape(1, HEAD_DIM))


def _memkv_kernel(mem_ref, g_ref, w_ref, km_ref, vm_ref):
    h = _rms(mem_ref[...], g_ref[...]).astype(BF16)
    kv = _dot(h, w_ref[...])
    km_ref[...] = kv[:, :M_Q].astype(km_ref.dtype)
    vm_ref[...] = kv[:, M_Q:].astype(vm_ref.dtype)


def _memkv(mem, g, w_bf):
    b, n, d = mem.shape
    blk = lambda i: (i, 0, 0)
    fixed = lambda i: (0, 0)
    return pl.pallas_call(
        _memkv_kernel,
        grid=(b,),
        in_specs=[pl.BlockSpec((None, n, d), blk), _resident((1, d), fixed), _resident(w_bf.shape, fixed)],
        out_specs=[pl.BlockSpec((None, n, M_Q), blk), pl.BlockSpec((None, n, M_Q), blk)],
        out_shape=[jax.ShapeDtypeStruct((b, n, M_Q), BF16)] * 2,
        compiler_params=_params("parallel"),
        name="memkv",
    )(mem, g.reshape(1, d), w_bf)


def _local_kernel(sink_ref, qa_ref, kp_ref, kc_ref, kn_ref, vp_ref, vc_ref, vn_ref,
                  qm_ref, km_ref, vm_ref, ga_ref, gm_ref, oa_ref, om_ref,
                  kbuf, vbuf, vmbuf, sbuf, smbuf, obuf, mbuf):
    bq = qa_ref.shape[0]
    nsub = bq // WINDOW
    i = pl.program_id(1)
    first = i == 0
    last = i == pl.num_programs(1) - 1
    group = A_HEADS // A_KV_HEADS
    cols = 3 * WINDOW
    two = 2 * HEAD_DIM

    kbuf[0:WINDOW] = kp_ref[...]
    kbuf[WINDOW:WINDOW + bq] = kc_ref[...]
    kbuf[WINDOW + bq:] = kn_ref[...]
    for g in range(A_KV_HEADS):
        sl = slice(g * HEAD_DIM, (g + 1) * HEAD_DIM)
        vbuf[0:WINDOW, g * two:g * two + HEAD_DIM] = vp_ref[:, sl]
        vbuf[WINDOW:WINDOW + bq, g * two:g * two + HEAD_DIM] = vc_ref[:, sl]
        vbuf[WINDOW + bq:, g * two:g * two + HEAD_DIM] = vn_ref[:, sl]
        vbuf[:, g * two + HEAD_DIM:(g + 1) * two] = jnp.ones((bq + 2 * WINDOW, HEAD_DIM), vbuf.dtype)
    for h in range(M_HEADS):
        vmbuf[:, h * two:h * two + HEAD_DIM] = vm_ref[:, h * HEAD_DIM:(h + 1) * HEAD_DIM]
        vmbuf[:, h * two + HEAD_DIM:(h + 1) * two] = jnp.ones((vm_ref.shape[0], HEAD_DIM), vmbuf.dtype)

    for j in range(nsub):
        for g in range(A_KV_HEADS):
            q = jnp.concatenate(
                [qa_ref[j * WINDOW:(j + 1) * WINDOW, (g * group + h) * HEAD_DIM:(g * group + h + 1) * HEAD_DIM]
                 for h in range(group)], axis=0)
            sbuf[j * A_KV_HEADS + g] = _dot_nt(q, kbuf[j * WINDOW:j * WINDOW + cols, g * HEAD_DIM:(g + 1) * HEAD_DIM])
    for h in range(M_HEADS):
        sl = slice(h * HEAD_DIM, (h + 1) * HEAD_DIM)
        smbuf[h] = _dot_nt(qm_ref[:, sl], km_ref[:, sl])

    qi = lax.broadcasted_iota(jnp.int32, (WINDOW, WINDOW), 0)
    kc = lax.broadcasted_iota(jnp.int32, (WINDOW, WINDOW), 1)
    neg = jnp.float32(NEG_INF)
    tile_heads = lambda t: jnp.concatenate([t] * group, axis=0)
    bias_left = jnp.where(kc >= qi, 0.0, neg)
    bias_right = jnp.where(kc <= qi, 0.0, neg)
    bias_left_first = tile_heads(jnp.where(first, neg, bias_left))
    bias_right_last = tile_heads(jnp.where(last, neg, bias_right))
    bias_left, bias_right = tile_heads(bias_left), tile_heads(bias_right)

    def softmax_pv(chunks, m, v):
        p = jnp.concatenate([jnp.exp2(c - m) for c in chunks], axis=1).astype(BF16)
        pv = _dot(p, v)
        return pv[:, :HEAD_DIM], pv[:, HEAD_DIM:]

    for j in range(nsub):
        for g in range(A_KV_HEADS):
            s = sbuf[j * A_KV_HEADS + g]
            chunks = [s[:, 0:WINDOW] + (bias_left_first if j == 0 else bias_left),
                      s[:, WINDOW:2 * WINDOW],
                      s[:, 2 * WINDOW:] + (bias_right_last if j == nsub - 1 else bias_right)]
            sink = jnp.concatenate(
                [jnp.full((WINDOW, HEAD_DIM), sink_ref[g * group + h] * LOG2E, F32) for h in range(group)], axis=0)
            m = jnp.maximum(jnp.max(functools.reduce(jnp.maximum, chunks), axis=-1, keepdims=True), sink)
            num, den = softmax_pv(chunks, m, vbuf[j * WINDOW:j * WINDOW + cols, g * two:(g + 1) * two])
            o = num * (1.0 / (den + jnp.exp2(sink - m)))
            for h in range(group):
                c0 = (g * group + h) * HEAD_DIM
                obuf[j * WINDOW:(j + 1) * WINDOW, c0:c0 + HEAD_DIM] = o[h * WINDOW:(h + 1) * WINDOW]
    oa_ref[...] = _rms(obuf[...], ga_ref[...]).astype(oa_ref.dtype)

    for h in range(M_HEADS):
        s = smbuf[h]
        chunks = [s[:, c * HEAD_DIM:(c + 1) * HEAD_DIM] for c in range(s.shape[1] // HEAD_DIM)]
        m = jnp.max(functools.reduce(jnp.maximum, chunks), axis=-1, keepdims=True)
        m = jnp.broadcast_to(m, chunks[0].shape)
        num, den = softmax_pv(chunks, m, vmbuf[:, h * two:(h + 1) * two])
        mbuf[:, h * HEAD_DIM:(h + 1) * HEAD_DIM] = num * (1.0 / den)
    om_ref[...] = _rms(mbuf[...], gm_ref[...]).astype(om_ref.dtype)


def _local_attention(qa, ka, va, qm, km, vm, sink, ga, gm, bq):
    b, seq, _ = qa.shape
    nt = seq // bq
    sub = bq // WINDOW
    nblk = seq // WINDOW
    n_mem = km.shape[1]
    tile = lambda bi, i: (bi, i, 0)
    prev = lambda bi, i: (bi, jnp.maximum(i * sub - 1, 0), 0)
    nxt = lambda bi, i: (bi, jnp.minimum((i + 1) * sub, nblk - 1), 0)
    per_b = lambda bi, i: (bi, 0, 0)
    fixed = lambda bi, i: (0, 0)
    kv_specs = [pl.BlockSpec((None, WINDOW, A_KV), prev), pl.BlockSpec((None, bq, A_KV), tile),
                pl.BlockSpec((None, WINDOW, A_KV), nxt)]
    return pl.pallas_call(
        _local_kernel,
        grid=(b, nt),
        in_specs=[pl.BlockSpec(memory_space=pltpu.SMEM),
                  pl.BlockSpec((None, bq, A_Q), tile)] + kv_specs + kv_specs + [
                  pl.BlockSpec((None, bq, M_Q), tile),
                  pl.BlockSpec((None, n_mem, M_Q), per_b), pl.BlockSpec((None, n_mem, M_Q), per_b),
                  _resident((1, A_Q), fixed), _resident((1, M_Q), fixed)],
        out_specs=[pl.BlockSpec((None, bq, A_Q), tile), pl.BlockSpec((None, bq, M_Q), tile)],
        out_shape=[jax.ShapeDtypeStruct((b, seq, A_Q), BF16), jax.ShapeDtypeStruct((b, seq, M_Q), BF16)],
        scratch_shapes=[pltpu.VMEM((bq + 2 * WINDOW, A_KV), BF16),
                        pltpu.VMEM((bq + 2 * WINDOW, 2 * A_KV), BF16),
                        pltpu.VMEM((n_mem, 2 * M_Q), BF16),
                        pltpu.VMEM((sub * A_KV_HEADS, (A_HEADS // A_KV_HEADS) * WINDOW, 3 * WINDOW), F32),
                        pltpu.VMEM((M_HEADS, bq, n_mem), F32),
                        pltpu.VMEM((bq, A_Q), F32), pltpu.VMEM((bq, M_Q), F32)],
        compiler_params=_params("parallel", "parallel"),
        name="local_attn",
    )(sink, qa, ka, ka, ka, va, va, va, qm, km, vm, ga.reshape(1, A_Q), gm.reshape(1, M_Q))


def _global_heads(q_ref, k_ref, v_ref, obuf, bk, bounded):
    bq = q_ref.shape[0]
    seq = k_ref.shape[0]
    group = B_HEADS // B_KV_HEADS
    rows = group * bq
    nchunk = bk // HEAD_DIM
    zeros = jnp.zeros((rows, HEAD_DIM), F32)

    for g in range(B_KV_HEADS):
        q = jnp.concatenate(
            [q_ref[:, (g * group + h) * HEAD_DIM:(g * group + h + 1) * HEAD_DIM] for h in range(group)], axis=0)
        ksl = slice(g * HEAD_DIM, (g + 1) * HEAD_DIM)

        def logits(j, q=q, ksl=ksl):
            start = pl.multiple_of(j * bk, bk)
            s = _dot_nt(q, k_ref[pl.ds(start, bk), ksl])
            return [s[:, c * HEAD_DIM:(c + 1) * HEAD_DIM] for c in range(nchunk)], v_ref[pl.ds(start, bk), ksl]

        def bounded_body(j, carry, logits=logits):
            l, acc = carry
            chunks, v = logits(j)
            ps = [jnp.exp2(c) for c in chunks]
            l = l + functools.reduce(jnp.add, ps)
            acc = acc + _dot(jnp.concatenate(ps, axis=1).astype(BF16), v)
            return l, acc

        def online_body(j, carry, logits=logits):
            m, l, acc = carry
            chunks, v = logits(j)
            m_new = jnp.maximum(m, jnp.max(functools.reduce(jnp.maximum, chunks), axis=-1, keepdims=True))
            alpha = jnp.exp2(m - m_new)
            ps = [jnp.exp2(c - m_new) for c in chunks]
            l = alpha * l + functools.reduce(jnp.add, ps)
            acc = alpha * acc + _dot(jnp.concatenate(ps, axis=1).astype(BF16), v)
            return m_new, l, acc

        if bounded:
            l, acc = lax.fori_loop(0, seq // bk, bounded_body, (zeros, zeros))
        else:
            _, l, acc = lax.fori_loop(0, seq // bk, online_body,
                                      (jnp.full((rows, HEAD_DIM), NEG_INF, F32), zeros, zeros))
        o = acc * (1.0 / jnp.sum(l, axis=-1, keepdims=True))
        for h in range(group):
            c0 = (g * group + h) * HEAD_DIM
            obuf[:, c0:c0 + HEAD_DIM] = o[h * bq:(h + 1) * bq]


def _global_kernel(bounded_ref, q_ref, k_ref, v_ref, g_ref, o_ref, obuf, *, bk):
    bounded = bounded_ref[0] == 1

    @pl.when(bounded)
    def _():
        _global_heads(q_ref, k_ref, v_ref, obuf, bk, True)

    @pl.when(jnp.logical_not(bounded))
    def _():
        _global_heads(q_ref, k_ref, v_ref, obuf, bk, False)

    o_ref[...] = _rms(obuf[...], g_ref[...]).astype(o_ref.dtype)


def _logits_bounded(qg, kg):
    bound = (HEAD_DIM ** 0.5) * jnp.max(jnp.abs(qg)) * jnp.max(jnp.abs(kg)) * ROUNDING_SLACK
    return (bound <= SOFTMAX_BOUND_LIMIT).astype(jnp.int32).reshape(1)


def _global_attention(q, k, v, gb, bounded, bq, bk):
    b, seq, _ = q.shape
    tile = lambda bi, i: (bi, i, 0)
    per_b = lambda bi, i: (bi, 0, 0)
    fixed = lambda bi, i: (0, 0)
    return pl.pallas_call(
        functools.partial(_global_kernel, bk=bk),
        grid=(b, seq // bq),
        in_specs=[pl.BlockSpec(memory_space=pltpu.SMEM),
                  pl.BlockSpec((None, bq, B_Q), tile),
                  _resident((None, seq, B_KV), per_b), _resident((None, seq, B_KV), per_b),
                  _resident((1, B_Q), fixed)],
        out_specs=pl.BlockSpec((None, bq, B_Q), tile),
        out_shape=jax.ShapeDtypeStruct((b, seq, B_Q), BF16),
        scratch_shapes=[pltpu.VMEM((bq, B_Q), F32)],
        compiler_params=_params("parallel", "parallel"),
        name="global_attn",
    )(bounded, q, k, v, gb.reshape(1, B_Q))


def _outproj_kernel(oa_ref, ob_ref, om_ref, x_ref, w_ref, x1_ref):
    merged = jnp.concatenate([oa_ref[...], ob_ref[...], om_ref[...]], axis=1)
    x1_ref[...] = x_ref[...] + _dot(merged, w_ref[...])


def _outproj(oa, ob, om, x2, w_bf, tm):
    tokens, d = x2.shape
    row = lambda i: (i, 0)
    fixed = lambda i: (0, 0)
    return pl.pallas_call(
        _outproj_kernel,
        grid=(tokens // tm,),
        in_specs=[pl.BlockSpec((tm, A_Q), row), pl.BlockSpec((tm, B_Q), row), pl.BlockSpec((tm, M_Q), row),
                  pl.BlockSpec((tm, d), row), _resident(w_bf.shape, fixed)],
        out_specs=pl.BlockSpec((tm, d), row),
        out_shape=jax.ShapeDtypeStruct((tokens, d), F32),
        compiler_params=_params("parallel"),
        name="outproj",
    )(oa, ob, om, x2, w_bf)


def _ffn_kernel(x1_ref, gf_ref, wg_ref, wu_ref, wd_ref, gl_ref, o_ref, h_scr):
    j = pl.program_id(1)

    @pl.when(j == 0)
    def _():
        x1 = x1_ref[...]
        h_scr[...] = _rms(x1, gf_ref[...]).astype(h_scr.dtype)
        o_ref[...] = x1

    h = h_scr[...]
    gate = _dot(h, wg_ref[...])
    up = _dot(h, wu_ref[...])
    act = (gate * jax.nn.sigmoid(gate) * up).astype(BF16)
    o_ref[...] += _dot(act, wd_ref[...])

    @pl.when(j == pl.num_programs(1) - 1)
    def _():
        o_ref[...] = _rms(o_ref[...], gl_ref[...])


def _ffn(x1, gf, w_gu_bf, w_dn_bf, gl, tm, tf):
    tokens, d = x1.shape
    d_ff = w_dn_bf.shape[0]
    nf = d_ff // tf
    row = lambda i, j: (i, 0)
    fixed = lambda i, j: (0, 0)
    return pl.pallas_call(
        _ffn_kernel,
        grid=(tokens // tm, nf),
        in_specs=[pl.BlockSpec((tm, d), row), _resident((1, d), fixed),
                  pl.BlockSpec((d, tf), lambda i, j: (0, j)), pl.BlockSpec((d, tf), lambda i, j: (0, j + nf)),
                  pl.BlockSpec((tf, d), lambda i, j: (j, 0)), _resident((1, d), fixed)],
        out_specs=pl.BlockSpec((tm, d), row),
        out_shape=jax.ShapeDtypeStruct((tokens, d), F32),
        scratch_shapes=[pltpu.VMEM((tm, d), BF16)],
        compiler_params=_params("parallel", "arbitrary"),
        name="ffn",
    )(x1, gf.reshape(1, d), w_gu_bf, w_gu_bf, w_dn_bf, gl.reshape(1, d))


def _tile(n, pref):
    t = min(pref, n)
    assert n % t == 0, (n, t)
    return t


def _tiles(seq, d_ff):
    return dict(
        proj=_tile(seq, 512),
        local=_tile(seq, 512),
        global_q=_tile(seq, 512),
        global_kv=_tile(seq, 2048),
        ffn=_tile(seq, 1024),
        ffn_cols=_tile(d_ff, 512),
    )


def _layer(x, mem, rope, norm_mix_g, norm_mem_g, w_in, w_mem_kv, sink_a, q_norm_b_g, k_norm_b_g,
           out_norm_g, w_out, norm_ffn_g, w_gate_up, w_down, final_g):
    b, seq, d = x.shape
    tokens = b * seq
    assert seq % GRID_W == 0 and seq % WINDOW == 0
    x2 = x.reshape(tokens, d)
    t = _tiles(seq, w_down.shape[0])

    qa, ka, va, qb, kb, vb, qm = _inproj(x2, seq, norm_mix_g, w_in.astype(BF16), q_norm_b_g, k_norm_b_g,
                                         rope, t["proj"])
    km, vm = _memkv(mem, norm_mem_g, w_mem_kv.astype(BF16))

    r3 = lambda a: a.reshape(b, seq, a.shape[-1])
    oa, om = _local_attention(r3(qa), r3(ka), r3(va), r3(qm), km, vm, sink_a,
                              out_norm_g[:A_Q], out_norm_g[A_Q + B_Q:], t["local"])
    ob = _global_attention(r3(qb), r3(kb), r3(vb), out_norm_g[A_Q:A_Q + B_Q],
                           _logits_bounded(q_norm_b_g, k_norm_b_g), t["global_q"], t["global_kv"])

    x1 = _outproj(oa.reshape(tokens, A_Q), ob.reshape(tokens, B_Q), om.reshape(tokens, M_Q), x2,
                  w_out.astype(BF16), t["proj"])
    y = _ffn(x1, norm_ffn_g, w_gate_up.astype(BF16), w_down.astype(BF16), final_g, t["ffn"], t["ffn_cols"])
    return y.reshape(b, seq, d)


def kernel(x_prompt, x_sample, mem_prompt, mem_sample, norm_mix_g, norm_mem_g, w_in, w_mem_kv, sink_a,
           q_norm_b_g, k_norm_b_g, out_norm_g, w_out, norm_ffn_g, w_gate_up, w_down, norm_final_g):
    depth = w_in.shape[0]
    assert depth == 1, "final norm is fused into the last layer's ffn kernel"
    args = (norm_mix_g[0], norm_mem_g[0], w_in[0], w_mem_kv[0], sink_a[0], q_norm_b_g[0], k_norm_b_g[0],
            out_norm_g[0], w_out[0], norm_ffn_g[0], w_gate_up[0], w_down[0], norm_final_g)
    rope = _rope_tables(max(x_prompt.shape[1], x_sample.shape[1]))
    return (_layer(x_prompt, mem_prompt, rope, *args), _layer(x_sample, mem_sample, rope, *args))
```

```python
import functools

import jax
import jax.numpy as jnp
from jax import lax
from jax.experimental import pallas as pl
from jax.experimental.pallas import tpu as pltpu

HEAD_DIM = 128
A_HEADS, A_KV_HEADS = 8, 2
B_HEADS, B_KV_HEADS = 4, 2
M_HEADS = 4
WINDOW = 128
GRID_W = 64
ROPE_THETA = 10000.0
NORM_EPS = 1e-6
NEG_INF = -1e30
A_Q, A_KV = A_HEADS * HEAD_DIM, A_KV_HEADS * HEAD_DIM
B_Q, B_KV = B_HEADS * HEAD_DIM, B_KV_HEADS * HEAD_DIM
M_Q = M_HEADS * HEAD_DIM
LOG2E = 1.4426950408889634
Q_SCALE = HEAD_DIM ** -0.5 * LOG2E
SOFTMAX_BOUND_LIMIT = 32.0
ROUNDING_SLACK = 1.02

BF16 = jnp.bfloat16
F32 = jnp.float32

VMEM_LIMIT_BYTES = 56 * 1024 * 1024


def _params(*sem):
    return pltpu.CompilerParams(dimension_semantics=sem, vmem_limit_bytes=VMEM_LIMIT_BYTES)


def _resident(shape, index_map):
    return pl.BlockSpec(shape, index_map, pipeline_mode=pl.Buffered(1))


def _rms(x, g):
    ms = jnp.mean(x * x, axis=-1, keepdims=True)
    return x * lax.rsqrt(ms + NORM_EPS) * g


def _dot(a, b):
    return jnp.dot(a, b, preferred_element_type=F32)


def _dot_nt(a, b):
    return lax.dot_general(a, b, (((1,), (1,)), ((), ())), preferred_element_type=F32)


def _inproj_kernel(x_ref, g_ref, w_ref, ca_ref, sa_ref, cb_ref, sb_ref, qg_ref, kg_ref,
                   qa_ref, ka_ref, va_ref, qb_ref, kb_ref, vbt_ref, qm_ref):
    h = _rms(x_ref[...], g_ref[...]).astype(BF16)
    ca, sa, cb, sb = ca_ref[...], sa_ref[...], cb_ref[...], sb_ref[...]
    lane = lax.broadcasted_iota(jnp.int32, ca.shape, 1)
    low_quarter = (lane % (HEAD_DIM // 2)) < (HEAD_DIM // 4)

    def rope_a(t):
        return t * ca + pltpu.roll(t, HEAD_DIM // 2, 1) * sa

    def rope_b(t):
        partner = jnp.where(low_quarter, pltpu.roll(t, HEAD_DIM - HEAD_DIM // 4, 1),
                            pltpu.roll(t, HEAD_DIM // 4, 1))
        return t * cb + partner * sb

    def heads(t, n, fn, out_ref):
        for i in range(n):
            sl = slice(i * HEAD_DIM, (i + 1) * HEAD_DIM)
            out_ref[:, sl] = fn(t[:, sl]).astype(out_ref.dtype)

    off = 0

    def section(width):
        nonlocal off
        t = _dot(h, w_ref[:, off:off + width])
        off += width
        return t

    heads(section(A_Q), A_HEADS, lambda t: rope_a(t) * Q_SCALE, qa_ref)
    heads(section(A_KV), A_KV_HEADS, rope_a, ka_ref)
    va_ref[...] = section(A_KV).astype(va_ref.dtype)
    qg, kg = qg_ref[...], kg_ref[...]
    heads(section(B_Q), B_HEADS, lambda t: rope_b(_rms(t, qg)) * Q_SCALE, qb_ref)
    heads(section(B_KV), B_KV_HEADS, lambda t: rope_b(_rms(t, kg)), kb_ref)
    vbt_ref[...] = section(B_KV).astype(vbt_ref.dtype).T
    qm_ref[...] = (section(M_Q) * Q_SCALE).astype(qm_ref.dtype)


def _rope_tables(seq):
    def tables(pos, dim):
        inv = ROPE_THETA ** (-(jnp.arange(0, dim, 2, dtype=F32) / dim))
        ang = pos[:, None] * inv[None, :]
        return jnp.cos(ang), jnp.sin(ang)

    c, s = tables(jnp.arange(seq, dtype=F32), HEAD_DIM)
    ca = jnp.concatenate([c, c], axis=-1)
    sa = jnp.concatenate([-s, s], axis=-1)
    n_rows = seq // GRID_W
    quarter = HEAD_DIM // 4
    per_row = lambda t: jnp.broadcast_to(t[:, None, :], (n_rows, GRID_W, quarter)).reshape(seq, quarter)
    per_col = lambda t: jnp.broadcast_to(t[None, :, :], (n_rows, GRID_W, quarter)).reshape(seq, quarter)
    cr, sr = (per_row(t) for t in tables(jnp.arange(n_rows, dtype=F32), HEAD_DIM // 2))
    cc, sc = (per_col(t) for t in tables(jnp.arange(GRID_W, dtype=F32), HEAD_DIM // 2))
    cb = jnp.concatenate([cr, cr, cc, cc], axis=-1)
    sb = jnp.concatenate([-sr, sr, -sc, sc], axis=-1)
    return ca, sa, cb, sb


def _inproj(x2, seq, g, w_bf, qg, kg, rope, tm, kv_block):
    tokens, d = x2.shape
    per_seq = seq // tm
    per_kv = kv_block // tm
    assert kv_block % tm == 0 and seq % kv_block == 0
    ca, sa, cb, sb = rope
    row = lambda i: (i, 0)
    fixed = lambda i: (0, 0)
    tab = lambda i: (i % per_seq, 0)
    widths = (A_Q, A_KV, A_KV, B_Q, B_KV, B_KV, M_Q)
    out_specs = [pl.BlockSpec((tm, w), row) for w in widths]
    out_shape = [jax.ShapeDtypeStruct((tokens, w), BF16) for w in widths]
    out_specs[5] = pl.BlockSpec((None, B_KV, tm), lambda i: (i // per_kv, 0, i % per_kv))
    out_shape[5] = jax.ShapeDtypeStruct((tokens // kv_block, B_KV, kv_block), BF16)
    return pl.pallas_call(
        _inproj_kernel,
        grid=(tokens // tm,),
        in_specs=[pl.BlockSpec((tm, d), row),
                  _resident((1, d), fixed),
                  _resident(w_bf.shape, fixed),
                  pl.BlockSpec((tm, HEAD_DIM), tab), pl.BlockSpec((tm, HEAD_DIM), tab),
                  pl.BlockSpec((tm, HEAD_DIM), tab), pl.BlockSpec((tm, HEAD_DIM), tab),
                  _resident((1, HEAD_DIM), fixed), _resident((1, HEAD_DIM), fixed)],
        out_specs=out_specs,
        out_shape=out_shape,
        compiler_params=_params("parallel"),
        name="inproj",
    )(x2, g.reshape(1, d), w_bf, ca, sa, cb, sb, qg.reshape(1, HEAD_DIM), kg.reshape(1, HEAD_DIM))


def _memkv_kernel(mem_ref, g_ref, w_ref, km_ref, vm_ref):
    h = _rms(mem_ref[...], g_ref[...]).astype(BF16)
    kv = _dot(h, w_ref[...])
    km_ref[...] = kv[:, :M_Q].astype(km_ref.dtype)
    vm_ref[...] = kv[:, M_Q:].astype(vm_ref.dtype)


def _memkv(mem, g, w_bf):
    b, n, d = mem.shape
    blk = lambda i: (i, 0, 0)
    fixed = lambda i: (0, 0)
    return pl.pallas_call(
        _memkv_kernel,
        grid=(b,),
        in_specs=[pl.BlockSpec((None, n, d), blk), _resident((1, d), fixed), _resident(w_bf.shape, fixed)],
        out_specs=[pl.BlockSpec((None, n, M_Q), blk), pl.BlockSpec((None, n, M_Q), blk)],
        out_shape=[jax.ShapeDtypeStruct((b, n, M_Q), BF16)] * 2,
        compiler_params=_params("parallel"),
        name="memkv",
    )(mem, g.reshape(1, d), w_bf)


def _local_kernel(sink_ref, qa_ref, kp_ref, kc_ref, kn_ref, vp_ref, vc_ref, vn_ref,
                  qm_ref, km_ref, vm_ref, ga_ref, gm_ref, oa_ref, om_ref,
                  kbuf, vbuf, vmbuf, sbuf, smbuf, obuf, mbuf):
    bq = qa_ref.shape[0]
    nsub = bq // WINDOW
    i = pl.program_id(1)
    first = i == 0
    last = i == pl.num_programs(1) - 1
    group = A_HEADS // A_KV_HEADS
    cols = 3 * WINDOW
    two = 2 * HEAD_DIM

    kbuf[0:WINDOW] = kp_ref[...]
    kbuf[WINDOW:WINDOW + bq] = kc_ref[...]
    kbuf[WINDOW + bq:] = kn_ref[...]
    for g in range(A_KV_HEADS):
        sl = slice(g * HEAD_DIM, (g + 1) * HEAD_DIM)
        vbuf[0:WINDOW, g * two:g * two + HEAD_DIM] = vp_ref[:, sl]
        vbuf[WINDOW:WINDOW + bq, g * two:g * two + HEAD_DIM] = vc_ref[:, sl]
        vbuf[WINDOW + bq:, g * two:g * two + HEAD_DIM] = vn_ref[:, sl]
        vbuf[:, g * two + HEAD_DIM:(g + 1) * two] = jnp.ones((bq + 2 * WINDOW, HEAD_DIM), vbuf.dtype)
    for h in range(M_HEADS):
        vmbuf[:, h * two:h * two + HEAD_DIM] = vm_ref[:, h * HEAD_DIM:(h + 1) * HEAD_DIM]
        vmbuf[:, h * two + HEAD_DIM:(h + 1) * two] = jnp.ones((vm_ref.shape[0], HEAD_DIM), vmbuf.dtype)

    for j in range(nsub):
        for g in range(A_KV_HEADS):
            q = jnp.concatenate(
                [qa_ref[j * WINDOW:(j + 1) * WINDOW, (g * group + h) * HEAD_DIM:(g * group + h + 1) * HEAD_DIM]
                 for h in range(group)], axis=0)
            sbuf[j * A_KV_HEADS + g] = _dot_nt(q, kbuf[j * WINDOW:j * WINDOW + cols, g * HEAD_DIM:(g + 1) * HEAD_DIM])
    for h in range(M_HEADS):
        sl = slice(h * HEAD_DIM, (h + 1) * HEAD_DIM)
        smbuf[h] = _dot_nt(qm_ref[:, sl], km_ref[:, sl])

    qi = lax.broadcasted_iota(jnp.int32, (WINDOW, WINDOW), 0)
    kc = lax.broadcasted_iota(jnp.int32, (WINDOW, WINDOW), 1)
    neg = jnp.float32(NEG_INF)
    tile_heads = lambda t: jnp.concatenate([t] * group, axis=0)
    bias_left = jnp.where(kc >= qi, 0.0, neg)
    bias_right = jnp.where(kc <= qi, 0.0, neg)
    bias_left_first = tile_heads(jnp.where(first, neg, bias_left))
    bias_right_last = tile_heads(jnp.where(last, neg, bias_right))
    bias_left, bias_right = tile_heads(bias_left), tile_heads(bias_right)

    def softmax_pv(chunks, m, v):
        p = jnp.concatenate([jnp.exp2(c - m) for c in chunks], axis=1).astype(BF16)
        pv = _dot(p, v)
        return pv[:, :HEAD_DIM], pv[:, HEAD_DIM:]

    for j in range(nsub):
        for g in range(A_KV_HEADS):
            s = sbuf[j * A_KV_HEADS + g]
            chunks = [s[:, 0:WINDOW] + (bias_left_first if j == 0 else bias_left),
                      s[:, WINDOW:2 * WINDOW],
                      s[:, 2 * WINDOW:] + (bias_right_last if j == nsub - 1 else bias_right)]
            sink = jnp.concatenate(
                [jnp.full((WINDOW, HEAD_DIM), sink_ref[g * group + h] * LOG2E, F32) for h in range(group)], axis=0)
            m = jnp.maximum(jnp.max(functools.reduce(jnp.maximum, chunks), axis=-1, keepdims=True), sink)
            num, den = softmax_pv(chunks, m, vbuf[j * WINDOW:j * WINDOW + cols, g * two:(g + 1) * two])
            o = num * (1.0 / (den + jnp.exp2(sink - m)))
            for h in range(group):
                c0 = (g * group + h) * HEAD_DIM
                obuf[j * WINDOW:(j + 1) * WINDOW, c0:c0 + HEAD_DIM] = o[h * WINDOW:(h + 1) * WINDOW]
    oa_ref[...] = _rms(obuf[...], ga_ref[...]).astype(oa_ref.dtype)

    for h in range(M_HEADS):
        s = smbuf[h]
        chunks = [s[:, c * HEAD_DIM:(c + 1) * HEAD_DIM] for c in range(s.shape[1] // HEAD_DIM)]
        m = jnp.max(functools.reduce(jnp.maximum, chunks), axis=-1, keepdims=True)
        m = jnp.broadcast_to(m, chunks[0].shape)
        num, den = softmax_pv(chunks, m, vmbuf[:, h * two:(h + 1) * two])
        mbuf[:, h * HEAD_DIM:(h + 1) * HEAD_DIM] = num * (1.0 / den)
    om_ref[...] = _rms(mbuf[...], gm_ref[...]).astype(om_ref.dtype)


def _local_attention(qa, ka, va, qm, km, vm, sink, ga, gm, bq):
    b, seq, _ = qa.shape
    nt = seq // bq
    sub = bq // WINDOW
    nblk = seq // WINDOW
    n_mem = km.shape[1]
    tile = lambda bi, i: (bi, i, 0)
    prev = lambda bi, i: (bi, jnp.maximum(i * sub - 1, 0), 0)
    nxt = lambda bi, i: (bi, jnp.minimum((i + 1) * sub, nblk - 1), 0)
    per_b = lambda bi, i: (bi, 0, 0)
    fixed = lambda bi, i: (0, 0)
    kv_specs = [pl.BlockSpec((None, WINDOW, A_KV), prev), pl.BlockSpec((None, bq, A_KV), tile),
                pl.BlockSpec((None, WINDOW, A_KV), nxt)]
    return pl.pallas_call(
        _local_kernel,
        grid=(b, nt),
        in_specs=[pl.BlockSpec(memory_space=pltpu.SMEM),
                  pl.BlockSpec((None, bq, A_Q), tile)] + kv_specs + kv_specs + [
                  pl.BlockSpec((None, bq, M_Q), tile),
                  pl.BlockSpec((None, n_mem, M_Q), per_b), pl.BlockSpec((None, n_mem, M_Q), per_b),
                  _resident((1, A_Q), fixed), _resident((1, M_Q), fixed)],
        out_specs=[pl.BlockSpec((None, bq, A_Q), tile), pl.BlockSpec((None, bq, M_Q), tile)],
        out_shape=[jax.ShapeDtypeStruct((b, seq, A_Q), BF16), jax.ShapeDtypeStruct((b, seq, M_Q), BF16)],
        scratch_shapes=[pltpu.VMEM((bq + 2 * WINDOW, A_KV), BF16),
                        pltpu.VMEM((bq + 2 * WINDOW, 2 * A_KV), BF16),
                        pltpu.VMEM((n_mem, 2 * M_Q), BF16),
                        pltpu.VMEM((sub * A_KV_HEADS, (A_HEADS // A_KV_HEADS) * WINDOW, 3 * WINDOW), F32),
                        pltpu.VMEM((M_HEADS, bq, n_mem), F32),
                        pltpu.VMEM((bq, A_Q), F32), pltpu.VMEM((bq, M_Q), F32)],
        compiler_params=_params("parallel", "parallel"),
        name="local_attn",
    )(sink, qa, ka, ka, ka, va, va, va, qm, km, vm, ga.reshape(1, A_Q), gm.reshape(1, M_Q))


def _global_heads(q_ref, k_ref, vt_ref, obuf, bounded):
    bq = q_ref.shape[0]
    n_kv, _, bk = vt_ref.shape
    group = B_HEADS // B_KV_HEADS
    rows = group * bq
    sub = 8
    l0 = jnp.zeros((sub, rows), F32)
    acc0 = jnp.zeros((HEAD_DIM, rows), F32)

    for g in range(B_KV_HEADS):
        q = jnp.concatenate(
            [q_ref[:, (g * group + h) * HEAD_DIM:(g * group + h + 1) * HEAD_DIM] for h in range(group)], axis=0)
        ksl = slice(g * HEAD_DIM, (g + 1) * HEAD_DIM)

        def logits_t(j, q=q, ksl=ksl):
            start = pl.multiple_of(j * bk, bk)
            return _dot_nt(k_ref[pl.ds(start, bk), ksl], q).reshape(bk // sub, sub, rows)

        def pv_t(j, pt, ksl=ksl):
            return _dot(vt_ref[j, ksl, :], pt.reshape(bk, rows).astype(BF16))

        def bounded_body(j, carry, logits_t=logits_t, pv_t=pv_t):
            l, acc = carry
            pt = jnp.exp2(logits_t(j))
            return l + jnp.sum(pt, axis=0), acc + pv_t(j, pt)

        def online_body(j, carry, logits_t=logits_t, pv_t=pv_t):
            m, l, acc = carry
            st = logits_t(j)
            m_new = jnp.maximum(m, jnp.max(jnp.max(st, axis=0), axis=0, keepdims=True))
            alpha = jnp.exp2(m - m_new)
            pt = jnp.exp2(st - m_new[None])
            return m_new, alpha * l + jnp.sum(pt, axis=0), alpha[:1] * acc + pv_t(j, pt)

        if bounded:
            l, acc = lax.fori_loop(0, n_kv, bounded_body, (l0, acc0))
        else:
            _, l, acc = lax.fori_loop(0, n_kv, online_body, (jnp.full((sub, rows), NEG_INF, F32), l0, acc0))
        o = (acc * (1.0 / jnp.sum(l, axis=0, keepdims=True))).T
        for h in range(group):
            c0 = (g * group + h) * HEAD_DIM
            obuf[:, c0:c0 + HEAD_DIM] = o[h * bq:(h + 1) * bq]


def _global_kernel(bounded_ref, q_ref, k_ref, vt_ref, g_ref, o_ref, obuf):
    bounded = bounded_ref[0] == 1

    @pl.when(bounded)
    def _():
        _global_heads(q_ref, k_ref, vt_ref, obuf, True)

    @pl.when(jnp.logical_not(bounded))
    def _():
        _global_heads(q_ref, k_ref, vt_ref, obuf, False)

    o_ref[...] = _rms(obuf[...], g_ref[...]).astype(o_ref.dtype)


def _logits_bounded(qg, kg):
    bound = (HEAD_DIM ** 0.5) * jnp.max(jnp.abs(qg)) * jnp.max(jnp.abs(kg)) * ROUNDING_SLACK
    return (bound <= SOFTMAX_BOUND_LIMIT).astype(jnp.int32).reshape(1)


def _global_attention(q, k, vt, gb, bounded, bq):
    b, seq, _ = q.shape
    n_kv, _, bk = vt.shape[1:]
    tile = lambda bi, i: (bi, i, 0)
    fixed = lambda bi, i: (0, 0)
    return pl.pallas_call(
        _global_kernel,
        grid=(b, seq // bq),
        in_specs=[pl.BlockSpec(memory_space=pltpu.SMEM),
                  pl.BlockSpec((None, bq, B_Q), tile),
                  _resident((None, seq, B_KV), lambda bi, i: (bi, 0, 0)),
                  _resident((None, n_kv, B_KV, bk), lambda bi, i: (bi, 0, 0, 0)),
                  _resident((1, B_Q), fixed)],
        out_specs=pl.BlockSpec((None, bq, B_Q), tile),
        out_shape=jax.ShapeDtypeStruct((b, seq, B_Q), BF16),
        scratch_shapes=[pltpu.VMEM((bq, B_Q), F32)],
        compiler_params=_params("parallel", "parallel"),
        name="global_attn",
    )(bounded, q, k, vt, gb.reshape(1, B_Q))


def _outproj_kernel(oa_ref, ob_ref, om_ref, x_ref, w_ref, x1_ref):
    merged = jnp.concatenate([oa_ref[...], ob_ref[...], om_ref[...]], axis=1)
    x1_ref[...] = x_ref[...] + _dot(merged, w_ref[...])


def _outproj(oa, ob, om, x2, w_bf, tm):
    tokens, d = x2.shape
    row = lambda i: (i, 0)
    fixed = lambda i: (0, 0)
    return pl.pallas_call(
        _outproj_kernel,
        grid=(tokens // tm,),
        in_specs=[pl.BlockSpec((tm, A_Q), row), pl.BlockSpec((tm, B_Q), row), pl.BlockSpec((tm, M_Q), row),
                  pl.BlockSpec((tm, d), row), _resident(w_bf.shape, fixed)],
        out_specs=pl.BlockSpec((tm, d), row),
        out_shape=jax.ShapeDtypeStruct((tokens, d), F32),
        compiler_params=_params("parallel"),
        name="outproj",
    )(oa, ob, om, x2, w_bf)


def _ffn_kernel(x1_ref, gf_ref, wg_ref, wu_ref, wd_ref, gl_ref, o_ref, h_scr):
    j = pl.program_id(1)

    @pl.when(j == 0)
    def _():
        x1 = x1_ref[...]
        h_scr[...] = _rms(x1, gf_ref[...]).astype(h_scr.dtype)
        o_ref[...] = x1

    h = h_scr[...]
    gate = _dot(h, wg_ref[...])
    up = _dot(h, wu_ref[...])
    act = (gate * jax.nn.sigmoid(gate) * up).astype(BF16)
    o_ref[...] += _dot(act, wd_ref[...])

    @pl.when(j == pl.num_programs(1) - 1)
    def _():
        o_ref[...] = _rms(o_ref[...], gl_ref[...])


def _ffn(x1, gf, w_gu_bf, w_dn_bf, gl, tm, tf):
    tokens, d = x1.shape
    d_ff = w_dn_bf.shape[0]
    nf = d_ff // tf
    row = lambda i, j: (i, 0)
    fixed = lambda i, j: (0, 0)
    return pl.pallas_call(
        _ffn_kernel,
        grid=(tokens // tm, nf),
        in_specs=[pl.BlockSpec((tm, d), row), _resident((1, d), fixed),
                  pl.BlockSpec((d, tf), lambda i, j: (0, j)), pl.BlockSpec((d, tf), lambda i, j: (0, j + nf)),
                  pl.BlockSpec((tf, d), lambda i, j: (j, 0)), _resident((1, d), fixed)],
        out_specs=pl.BlockSpec((tm, d), row),
        out_shape=jax.ShapeDtypeStruct((tokens, d), F32),
        scratch_shapes=[pltpu.VMEM((tm, d), BF16)],
        compiler_params=_params("parallel", "arbitrary"),
        name="ffn",
    )(x1, gf.reshape(1, d), w_gu_bf, w_gu_bf, w_dn_bf, gl.reshape(1, d))


def _tile(n, pref):
    t = min(pref, n)
    assert n % t == 0, (n, t)
    return t


def _tiles(seq, d_ff):
    return dict(
        proj=_tile(seq, 512),
        local=_tile(seq, 512),
        global_q=_tile(seq, 512),
        global_kv=_tile(seq, 2048),
        ffn=_tile(seq, 1024),
        ffn_cols=_tile(d_ff, 512),
    )


def _layer(x, mem, rope, norm_mix_g, norm_mem_g, w_in, w_mem_kv, sink_a, q_norm_b_g, k_norm_b_g,
           out_norm_g, w_out, norm_ffn_g, w_gate_up, w_down, final_g):
    b, seq, d = x.shape
    tokens = b * seq
    assert seq % GRID_W == 0 and seq % WINDOW == 0
    x2 = x.reshape(tokens, d)
    t = _tiles(seq, w_down.shape[0])

    qa, ka, va, qb, kb, vbt, qm = _inproj(x2, seq, norm_mix_g, w_in.astype(BF16), q_norm_b_g, k_norm_b_g,
                                          rope, t["proj"], t["global_kv"])
    km, vm = _memkv(mem, norm_mem_g, w_mem_kv.astype(BF16))

    r3 = lambda a: a.reshape(b, seq, a.shape[-1])
    oa, om = _local_attention(r3(qa), r3(ka), r3(va), r3(qm), km, vm, sink_a,
                              out_norm_g[:A_Q], out_norm_g[A_Q + B_Q:], t["local"])
    vbt = vbt.reshape(b, seq // t["global_kv"], B_KV, t["global_kv"])
    ob = _global_attention(r3(qb), r3(kb), vbt, out_norm_g[A_Q:A_Q + B_Q],
                           _logits_bounded(q_norm_b_g, k_norm_b_g), t["global_q"])

    x1 = _outproj(oa.reshape(tokens, A_Q), ob.reshape(tokens, B_Q), om.reshape(tokens, M_Q), x2,
                  w_out.astype(BF16), t["proj"])
    y = _ffn(x1, norm_ffn_g, w_gate_up.astype(BF16), w_down.astype(BF16), final_g, t["ffn"], t["ffn_cols"])
    return y.reshape(b, seq, d)


def kernel(x_prompt, x_sample, mem_prompt, mem_sample, norm_mix_g, norm_mem_g, w_in, w_mem_kv, sink_a,
           q_norm_b_g, k_norm_b_g, out_norm_g, w_out, norm_ffn_g, w_gate_up, w_down, norm_final_g):
    depth = w_in.shape[0]
    assert depth == 1, "final norm is fused into the last layer's ffn kernel"
    args = (norm_mix_g[0], norm_mem_g[0], w_in[0], w_mem_kv[0], sink_a[0], q_norm_b_g[0], k_norm_b_g[0],
            out_norm_g[0], w_out[0], norm_ffn_g[0], w_gate_up[0], w_down[0], norm_final_g)
    rope = _rope_tables(max(x_prompt.shape[1], x_sample.shape[1]))
    return (_layer(x_prompt, mem_prompt, rope, *args), _layer(x_sample, mem_sample, rope, *args))
```

```python
import functools

import jax
import jax.numpy as jnp
from jax import lax
from jax.experimental import pallas as pl
from jax.experimental.pallas import tpu as pltpu

HEAD_DIM = 128
A_HEADS, A_KV_HEADS = 8, 2
B_HEADS, B_KV_HEADS = 4, 2
M_HEADS = 4
WINDOW = 128
GRID_W = 64
ROPE_THETA = 10000.0
NORM_EPS = 1e-6
NEG_INF = -1e30
A_Q, A_KV = A_HEADS * HEAD_DIM, A_KV_HEADS * HEAD_DIM
B_Q, B_KV = B_HEADS * HEAD_DIM, B_KV_HEADS * HEAD_DIM
M_Q = M_HEADS * HEAD_DIM
LOG2E = 1.4426950408889634
Q_SCALE = HEAD_DIM ** -0.5 * LOG2E
SOFTMAX_BOUND_LIMIT = 32.0
ROUNDING_SLACK = 1.02

BF16 = jnp.bfloat16
F32 = jnp.float32

VMEM_LIMIT_BYTES = 56 * 1024 * 1024


def _params(*sem):
    return pltpu.CompilerParams(dimension_semantics=sem, vmem_limit_bytes=VMEM_LIMIT_BYTES)


def _resident(shape, index_map):
    return pl.BlockSpec(shape, index_map, pipeline_mode=pl.Buffered(1))


def _rms(x, g):
    ms = jnp.mean(x * x, axis=-1, keepdims=True)
    return x * lax.rsqrt(ms + NORM_EPS) * g


def _dot(a, b):
    return jnp.dot(a, b, preferred_element_type=F32)


def _dot_nt(a, b):
    return lax.dot_general(a, b, (((1,), (1,)), ((), ())), preferred_element_type=F32)


def _inproj_kernel(x_ref, g_ref, w_ref, ca_ref, sa_ref, cb_ref, sb_ref, qg_ref, kg_ref,
                   qa_ref, ka_ref, va_ref, qb_ref, kb_ref, vbt_ref, qm_ref):
    h = _rms(x_ref[...], g_ref[...]).astype(BF16)
    ca, sa, cb, sb = ca_ref[...], sa_ref[...], cb_ref[...], sb_ref[...]
    lane = lax.broadcasted_iota(jnp.int32, ca.shape, 1)
    low_quarter = (lane % (HEAD_DIM // 2)) < (HEAD_DIM // 4)

    def rope_a(t):
        return t * ca + pltpu.roll(t, HEAD_DIM // 2, 1) * sa

    def rope_b(t):
        partner = jnp.where(low_quarter, pltpu.roll(t, HEAD_DIM - HEAD_DIM // 4, 1),
                            pltpu.roll(t, HEAD_DIM // 4, 1))
        return t * cb + partner * sb

    def heads(t, n, fn, out_ref):
        for i in range(n):
            sl = slice(i * HEAD_DIM, (i + 1) * HEAD_DIM)
            out_ref[:, sl] = fn(t[:, sl]).astype(out_ref.dtype)

    widths = (A_Q, A_KV, A_KV, B_Q, B_KV, B_KV, M_Q)
    starts = [sum(widths[:i]) for i in range(len(widths))]

    def section(i):
        return _dot(h, w_ref[:, starts[i]:starts[i] + widths[i]])

    qg, kg = qg_ref[...], kg_ref[...]
    heads(section(3), B_HEADS, lambda t: rope_b(_rms(t, qg)) * Q_SCALE, qb_ref)
    heads(section(4), B_KV_HEADS, lambda t: rope_b(_rms(t, kg)), kb_ref)
    vbt_ref[...] = section(5).astype(vbt_ref.dtype).T
    heads(section(0), A_HEADS, lambda t: rope_a(t) * Q_SCALE, qa_ref)
    heads(section(1), A_KV_HEADS, rope_a, ka_ref)
    qm_ref[...] = (section(6) * Q_SCALE).astype(qm_ref.dtype)
    va_ref[...] = section(2).astype(va_ref.dtype)


def _rope_tables(seq):
    def tables(n, dim):
        inv = ROPE_THETA ** (-(jnp.arange(0, dim, 2, dtype=F32) / dim))
        ang = jnp.arange(n, dtype=F32)[:, None] * jnp.tile(inv, 2)[None, :]
        sign = jnp.where(jnp.arange(dim) < dim // 2, -1.0, 1.0).astype(F32)
        return jnp.cos(ang), jnp.sin(ang) * sign

    ca, sa = tables(seq, HEAD_DIM)
    n_rows = seq // GRID_W
    half = HEAD_DIM // 2
    per_row = lambda t: jnp.broadcast_to(t[:, None, :], (n_rows, GRID_W, half))
    per_col = lambda t: jnp.broadcast_to(t[None, :, :], (n_rows, GRID_W, half))
    cr, sr = tables(n_rows, half)
    cc, sc = tables(GRID_W, half)
    cb = jnp.concatenate([per_row(cr), per_col(cc)], axis=-1).reshape(seq, HEAD_DIM)
    sb = jnp.concatenate([per_row(sr), per_col(sc)], axis=-1).reshape(seq, HEAD_DIM)
    return ca, sa, cb, sb


def _inproj(x2, seq, g, w_bf, qg, kg, rope, tm, kv_block):
    tokens, d = x2.shape
    per_seq = seq // tm
    per_kv = kv_block // tm
    assert kv_block % tm == 0 and seq % kv_block == 0
    ca, sa, cb, sb = rope
    row = lambda i: (i, 0)
    fixed = lambda i: (0, 0)
    tab = lambda i: (i % per_seq, 0)
    widths = (A_Q, A_KV, A_KV, B_Q, B_KV, B_KV, M_Q)
    out_specs = [pl.BlockSpec((tm, w), row) for w in widths]
    out_shape = [jax.ShapeDtypeStruct((tokens, w), BF16) for w in widths]
    out_specs[5] = pl.BlockSpec((None, B_KV, tm), lambda i: (i // per_kv, 0, i % per_kv))
    out_shape[5] = jax.ShapeDtypeStruct((tokens // kv_block, B_KV, kv_block), BF16)
    return pl.pallas_call(
        _inproj_kernel,
        grid=(tokens // tm,),
        in_specs=[pl.BlockSpec((tm, d), row),
                  _resident((1, d), fixed),
                  _resident(w_bf.shape, fixed),
                  pl.BlockSpec((tm, HEAD_DIM), tab), pl.BlockSpec((tm, HEAD_DIM), tab),
                  pl.BlockSpec((tm, HEAD_DIM), tab), pl.BlockSpec((tm, HEAD_DIM), tab),
                  _resident((1, HEAD_DIM), fixed), _resident((1, HEAD_DIM), fixed)],
        out_specs=out_specs,
        out_shape=out_shape,
        compiler_params=_params("parallel"),
        name="inproj",
    )(x2, g.reshape(1, d), w_bf, ca, sa, cb, sb, qg.reshape(1, HEAD_DIM), kg.reshape(1, HEAD_DIM))


def _memkv_kernel(mem_ref, g_ref, w_ref, km_ref, vm_ref):
    h = _rms(mem_ref[...], g_ref[...]).astype(BF16)
    kv = _dot(h, w_ref[...])
    km_ref[...] = kv[:, :M_Q].astype(km_ref.dtype)
    vm_ref[...] = kv[:, M_Q:].astype(vm_ref.dtype)


def _memkv(mem, g, w_bf):
    b, n, d = mem.shape
    blk = lambda i: (i, 0, 0)
    fixed = lambda i: (0, 0)
    return pl.pallas_call(
        _memkv_kernel,
        grid=(b,),
        in_specs=[pl.BlockSpec((None, n, d), blk), _resident((1, d), fixed), _resident(w_bf.shape, fixed)],
        out_specs=[pl.BlockSpec((None, n, M_Q), blk), pl.BlockSpec((None, n, M_Q), blk)],
        out_shape=[jax.ShapeDtypeStruct((b, n, M_Q), BF16)] * 2,
        compiler_params=_params("parallel"),
        name="memkv",
    )(mem, g.reshape(1, d), w_bf)


def _local_kernel(sink_ref, qa_ref, kp_ref, kc_ref, kn_ref, vp_ref, vc_ref, vn_ref,
                  qm_ref, km_ref, vm_ref, ga_ref, gm_ref, oa_ref, om_ref,
                  kbuf, vbuf, vmbuf, sbuf, smbuf, obuf, mbuf):
    bq = qa_ref.shape[0]
    nsub = bq // WINDOW
    i = pl.program_id(1)
    first = i == 0
    last = i == pl.num_programs(1) - 1
    group = A_HEADS // A_KV_HEADS
    cols = 3 * WINDOW
    two = 2 * HEAD_DIM

    kbuf[0:WINDOW] = kp_ref[...]
    kbuf[WINDOW:WINDOW + bq] = kc_ref[...]
    kbuf[WINDOW + bq:] = kn_ref[...]
    for g in range(A_KV_HEADS):
        sl = slice(g * HEAD_DIM, (g + 1) * HEAD_DIM)
        vbuf[0:WINDOW, g * two:g * two + HEAD_DIM] = vp_ref[:, sl]
        vbuf[WINDOW:WINDOW + bq, g * two:g * two + HEAD_DIM] = vc_ref[:, sl]
        vbuf[WINDOW + bq:, g * two:g * two + HEAD_DIM] = vn_ref[:, sl]
        vbuf[:, g * two + HEAD_DIM:(g + 1) * two] = jnp.ones((bq + 2 * WINDOW, HEAD_DIM), vbuf.dtype)
    for h in range(M_HEADS):
        vmbuf[:, h * two:h * two + HEAD_DIM] = vm_ref[:, h * HEAD_DIM:(h + 1) * HEAD_DIM]
        vmbuf[:, h * two + HEAD_DIM:(h + 1) * two] = jnp.ones((vm_ref.shape[0], HEAD_DIM), vmbuf.dtype)

    for j in range(nsub):
        for g in range(A_KV_HEADS):
            q = jnp.concatenate(
                [qa_ref[j * WINDOW:(j + 1) * WINDOW, (g * group + h) * HEAD_DIM:(g * group + h + 1) * HEAD_DIM]
                 for h in range(group)], axis=0)
            sbuf[j * A_KV_HEADS + g] = _dot_nt(q, kbuf[j * WINDOW:j * WINDOW + cols, g * HEAD_DIM:(g + 1) * HEAD_DIM])
    for h in range(M_HEADS):
        sl = slice(h * HEAD_DIM, (h + 1) * HEAD_DIM)
        smbuf[h] = _dot_nt(qm_ref[:, sl], km_ref[:, sl])

    qi = lax.broadcasted_iota(jnp.int32, (WINDOW, WINDOW), 0)
    kc = lax.broadcasted_iota(jnp.int32, (WINDOW, WINDOW), 1)
    neg = jnp.float32(NEG_INF)
    tile_heads = lambda t: jnp.concatenate([t] * group, axis=0)
    bias_left = jnp.where(kc >= qi, 0.0, neg)
    bias_right = jnp.where(kc <= qi, 0.0, neg)
    bias_left_first = tile_heads(jnp.where(first, neg, bias_left))
    bias_right_last = tile_heads(jnp.where(last, neg, bias_right))
    bias_left, bias_right = tile_heads(bias_left), tile_heads(bias_right)

    def softmax_pv(chunks, m, v):
        p = jnp.concatenate([jnp.exp2(c - m) for c in chunks], axis=1).astype(BF16)
        pv = _dot(p, v)
        return pv[:, :HEAD_DIM], pv[:, HEAD_DIM:]

    for j in range(nsub):
        for g in range(A_KV_HEADS):
            s = sbuf[j * A_KV_HEADS + g]
            chunks = [s[:, 0:WINDOW] + (bias_left_first if j == 0 else bias_left),
                      s[:, WINDOW:2 * WINDOW],
                      s[:, 2 * WINDOW:] + (bias_right_last if j == nsub - 1 else bias_right)]
            sink = jnp.concatenate(
                [jnp.full((WINDOW, HEAD_DIM), sink_ref[g * group + h] * LOG2E, F32) for h in range(group)], axis=0)
            m = jnp.maximum(jnp.max(functools.reduce(jnp.maximum, chunks), axis=-1, keepdims=True), sink)
            num, den = softmax_pv(chunks, m, vbuf[j * WINDOW:j * WINDOW + cols, g * two:(g + 1) * two])
            o = num * (1.0 / (den + jnp.exp2(sink - m)))
            for h in range(group):
                c0 = (g * group + h) * HEAD_DIM
                obuf[j * WINDOW:(j + 1) * WINDOW, c0:c0 + HEAD_DIM] = o[h * WINDOW:(h + 1) * WINDOW]
    oa_ref[...] = _rms(obuf[...], ga_ref[...]).astype(oa_ref.dtype)

    for h in range(M_HEADS):
        s = smbuf[h]
        chunks = [s[:, c * HEAD_DIM:(c + 1) * HEAD_DIM] for c in range(s.shape[1] // HEAD_DIM)]
        m = jnp.max(functools.reduce(jnp.maximum, chunks), axis=-1, keepdims=True)
        m = jnp.broadcast_to(m, chunks[0].shape)
        num, den = softmax_pv(chunks, m, vmbuf[:, h * two:(h + 1) * two])
        mbuf[:, h * HEAD_DIM:(h + 1) * HEAD_DIM] = num * (1.0 / den)
    om_ref[...] = _rms(mbuf[...], gm_ref[...]).astype(om_ref.dtype)


def _local_attention(qa, ka, va, qm, km, vm, sink, ga, gm, bq):
    b, seq, _ = qa.shape
    nt = seq // bq
    sub = bq // WINDOW
    nblk = seq // WINDOW
    n_mem = km.shape[1]
    tile = lambda bi, i: (bi, i, 0)
    prev = lambda bi, i: (bi, jnp.maximum(i * sub - 1, 0), 0)
    nxt = lambda bi, i: (bi, jnp.minimum((i + 1) * sub, nblk - 1), 0)
    per_b = lambda bi, i: (bi, 0, 0)
    fixed = lambda bi, i: (0, 0)
    kv_specs = [pl.BlockSpec((None, WINDOW, A_KV), prev), pl.BlockSpec((None, bq, A_KV), tile),
                pl.BlockSpec((None, WINDOW, A_KV), nxt)]
    return pl.pallas_call(
        _local_kernel,
        grid=(b, nt),
        in_specs=[pl.BlockSpec(memory_space=pltpu.SMEM),
                  pl.BlockSpec((None, bq, A_Q), tile)] + kv_specs + kv_specs + [
                  pl.BlockSpec((None, bq, M_Q), tile),
                  pl.BlockSpec((None, n_mem, M_Q), per_b), pl.BlockSpec((None, n_mem, M_Q), per_b),
                  _resident((1, A_Q), fixed), _resident((1, M_Q), fixed)],
        out_specs=[pl.BlockSpec((None, bq, A_Q), tile), pl.BlockSpec((None, bq, M_Q), tile)],
        out_shape=[jax.ShapeDtypeStruct((b, seq, A_Q), BF16), jax.ShapeDtypeStruct((b, seq, M_Q), BF16)],
        scratch_shapes=[pltpu.VMEM((bq + 2 * WINDOW, A_KV), BF16),
                        pltpu.VMEM((bq + 2 * WINDOW, 2 * A_KV), BF16),
                        pltpu.VMEM((n_mem, 2 * M_Q), BF16),
                        pltpu.VMEM((sub * A_KV_HEADS, (A_HEADS // A_KV_HEADS) * WINDOW, 3 * WINDOW), F32),
                        pltpu.VMEM((M_HEADS, bq, n_mem), F32),
                        pltpu.VMEM((bq, A_Q), F32), pltpu.VMEM((bq, M_Q), F32)],
        compiler_params=_params("parallel", "parallel"),
        name="local_attn",
    )(sink, qa, ka, ka, ka, va, va, va, qm, km, vm, ga.reshape(1, A_Q), gm.reshape(1, M_Q))


def _global_heads(q_ref, k_ref, vt_ref, obuf, bounded):
    bq = q_ref.shape[0]
    n_kv, _, bk = vt_ref.shape
    group = B_HEADS // B_KV_HEADS
    rows = group * bq
    sub = 8
    l0 = jnp.zeros((sub, rows), F32)
    acc0 = jnp.zeros((HEAD_DIM, rows), F32)

    for g in range(B_KV_HEADS):
        q = jnp.concatenate(
            [q_ref[:, (g * group + h) * HEAD_DIM:(g * group + h + 1) * HEAD_DIM] for h in range(group)], axis=0)
        ksl = slice(g * HEAD_DIM, (g + 1) * HEAD_DIM)

        def logits_t(j, q=q, ksl=ksl):
            start = pl.multiple_of(j * bk, bk)
            return _dot_nt(k_ref[pl.ds(start, bk), ksl], q).reshape(bk // sub, sub, rows)

        def pv_t(j, pt, ksl=ksl):
            return _dot(vt_ref[j, ksl, :], pt.reshape(bk, rows).astype(BF16))

        def bounded_body(j, carry, logits_t=logits_t, pv_t=pv_t):
            l, acc = carry
            pt = jnp.exp2(logits_t(j))
            return l + jnp.sum(pt, axis=0), acc + pv_t(j, pt)

        def online_body(j, carry, logits_t=logits_t, pv_t=pv_t):
            m, l, acc = carry
            st = logits_t(j)
            m_new = jnp.maximum(m, jnp.max(jnp.max(st, axis=0), axis=0, keepdims=True))
            alpha = jnp.exp2(m - m_new)
            pt = jnp.exp2(st - m_new[None])
            return m_new, alpha * l + jnp.sum(pt, axis=0), alpha[:1] * acc + pv_t(j, pt)

        if bounded:
            l, acc = lax.fori_loop(0, n_kv, bounded_body, (l0, acc0))
        else:
            _, l, acc = lax.fori_loop(0, n_kv, online_body, (jnp.full((sub, rows), NEG_INF, F32), l0, acc0))
        o = (acc * (1.0 / jnp.sum(l, axis=0, keepdims=True))).T
        for h in range(group):
            c0 = (g * group + h) * HEAD_DIM
            obuf[:, c0:c0 + HEAD_DIM] = o[h * bq:(h + 1) * bq]


def _global_kernel(bounded_ref, q_ref, k_ref, vt_ref, g_ref, o_ref, obuf):
    bounded = bounded_ref[0] == 1

    @pl.when(bounded)
    def _():
        _global_heads(q_ref, k_ref, vt_ref, obuf, True)

    @pl.when(jnp.logical_not(bounded))
    def _():
        _global_heads(q_ref, k_ref, vt_ref, obuf, False)

    o_ref[...] = _rms(obuf[...], g_ref[...]).astype(o_ref.dtype)


def _logits_bounded(qg, kg):
    bound = (HEAD_DIM ** 0.5) * jnp.max(jnp.abs(qg)) * jnp.max(jnp.abs(kg)) * ROUNDING_SLACK
    return (bound <= SOFTMAX_BOUND_LIMIT).astype(jnp.int32).reshape(1)


def _global_attention(q, k, vt, gb, bounded, bq):
    b, seq, _ = q.shape
    n_kv, _, bk = vt.shape[1:]
    tile = lambda bi, i: (bi, i, 0)
    fixed = lambda bi, i: (0, 0)
    return pl.pallas_call(
        _global_kernel,
        grid=(b, seq // bq),
        in_specs=[pl.BlockSpec(memory_space=pltpu.SMEM),
                  pl.BlockSpec((None, bq, B_Q), tile),
                  _resident((None, seq, B_KV), lambda bi, i: (bi, 0, 0)),
                  _resident((None, n_kv, B_KV, bk), lambda bi, i: (bi, 0, 0, 0)),
                  _resident((1, B_Q), fixed)],
        out_specs=pl.BlockSpec((None, bq, B_Q), tile),
        out_shape=jax.ShapeDtypeStruct((b, seq, B_Q), BF16),
        scratch_shapes=[pltpu.VMEM((bq, B_Q), F32)],
        compiler_params=_params("parallel", "parallel"),
        name="global_attn",
    )(bounded, q, k, vt, gb.reshape(1, B_Q))


def _outproj_kernel(oa_ref, ob_ref, om_ref, x_ref, w_ref, x1_ref):
    merged = jnp.concatenate([oa_ref[...], ob_ref[...], om_ref[...]], axis=1)
    x1_ref[...] = x_ref[...] + _dot(merged, w_ref[...])


def _outproj(oa, ob, om, x2, w_bf, tm):
    tokens, d = x2.shape
    row = lambda i: (i, 0)
    fixed = lambda i: (0, 0)
    return pl.pallas_call(
        _outproj_kernel,
        grid=(tokens // tm,),
        in_specs=[pl.BlockSpec((tm, A_Q), row), pl.BlockSpec((tm, B_Q), row), pl.BlockSpec((tm, M_Q), row),
                  pl.BlockSpec((tm, d), row), _resident(w_bf.shape, fixed)],
        out_specs=pl.BlockSpec((tm, d), row),
        out_shape=jax.ShapeDtypeStruct((tokens, d), F32),
        compiler_params=_params("parallel"),
        name="outproj",
    )(oa, ob, om, x2, w_bf)


def _ffn_kernel(x1_ref, gf_ref, wg_ref, wu_ref, wd_ref, gl_ref, o_ref, h_scr):
    j = pl.program_id(1)

    @pl.when(j == 0)
    def _():
        x1 = x1_ref[...]
        h_scr[...] = _rms(x1, gf_ref[...]).astype(h_scr.dtype)
        o_ref[...] = x1

    h = h_scr[...]
    gate = _dot(h, wg_ref[...])
    up = _dot(h, wu_ref[...])
    act = (gate * jax.nn.sigmoid(gate) * up).astype(BF16)
    o_ref[...] += _dot(act, wd_ref[...])

    @pl.when(j == pl.num_programs(1) - 1)
    def _():
        o_ref[...] = _rms(o_ref[...], gl_ref[...])


def _ffn(x1, gf, w_gu_bf, w_dn_bf, gl, tm, tf):
    tokens, d = x1.shape
    d_ff = w_dn_bf.shape[0]
    nf = d_ff // tf
    row = lambda i, j: (i, 0)
    fixed = lambda i, j: (0, 0)
    return pl.pallas_call(
        _ffn_kernel,
        grid=(tokens // tm, nf),
        in_specs=[pl.BlockSpec((tm, d), row), _resident((1, d), fixed),
                  pl.BlockSpec((d, tf), lambda i, j: (0, j)), pl.BlockSpec((d, tf), lambda i, j: (0, j + nf)),
                  pl.BlockSpec((tf, d), lambda i, j: (j, 0)), _resident((1, d), fixed)],
        out_specs=pl.BlockSpec((tm, d), row),
        out_shape=jax.ShapeDtypeStruct((tokens, d), F32),
        scratch_shapes=[pltpu.VMEM((tm, d), BF16)],
        compiler_params=_params("parallel", "arbitrary"),
        name="ffn",
    )(x1, gf.reshape(1, d), w_gu_bf, w_gu_bf, w_dn_bf, gl.reshape(1, d))


def _tile(n, pref):
    t = min(pref, n)
    assert n % t == 0, (n, t)
    return t


def _tiles(seq, d_ff):
    return dict(
        proj=_tile(seq, 512),
        local=_tile(seq, 512),
        global_q=_tile(seq, 512),
        global_kv=_tile(seq, 4096),
        ffn=_tile(seq, 1024),
        ffn_cols=_tile(d_ff, 512),
    )


def _layer(x, mem, rope, norm_mix_g, norm_mem_g, w_in, w_mem_kv, sink_a, q_norm_b_g, k_norm_b_g,
           out_norm_g, w_out, norm_ffn_g, w_gate_up, w_down, final_g):
    b, seq, d = x.shape
    tokens = b * seq
    assert seq % GRID_W == 0 and seq % WINDOW == 0
    x2 = x.reshape(tokens, d)
    t = _tiles(seq, w_down.shape[0])

    qa, ka, va, qb, kb, vbt, qm = _inproj(x2, seq, norm_mix_g, w_in.astype(BF16), q_norm_b_g, k_norm_b_g,
                                          rope, t["proj"], t["global_kv"])
    km, vm = _memkv(mem, norm_mem_g, w_mem_kv.astype(BF16))

    r3 = lambda a: a.reshape(b, seq, a.shape[-1])
    oa, om = _local_attention(r3(qa), r3(ka), r3(va), r3(qm), km, vm, sink_a,
                              out_norm_g[:A_Q], out_norm_g[A_Q + B_Q:], t["local"])
    vbt = vbt.reshape(b, seq // t["global_kv"], B_KV, t["global_kv"])
    ob = _global_attention(r3(qb), r3(kb), vbt, out_norm_g[A_Q:A_Q + B_Q],
                           _logits_bounded(q_norm_b_g, k_norm_b_g), t["global_q"])

    x1 = _outproj(oa.reshape(tokens, A_Q), ob.reshape(tokens, B_Q), om.reshape(tokens, M_Q), x2,
                  w_out.astype(BF16), t["proj"])
    y = _ffn(x1, norm_ffn_g, w_gate_up.astype(BF16), w_down.astype(BF16), final_g, t["ffn"], t["ffn_cols"])
    return y.reshape(b, seq, d)


def kernel(x_prompt, x_sample, mem_prompt, mem_sample, norm_mix_g, norm_mem_g, w_in, w_mem_kv, sink_a,
           q_norm_b_g, k_norm_b_g, out_norm_g, w_out, norm_ffn_g, w_gate_up, w_down, norm_final_g):
    depth = w_in.shape[0]
    assert depth == 1, "final norm is fused into the last layer's ffn kernel"
    args = (norm_mix_g[0], norm_mem_g[0], w_in[0], w_mem_kv[0], sink_a[0], q_norm_b_g[0], k_norm_b_g[0],
            out_norm_g[0], w_out[0], norm_ffn_g[0], w_gate_up[0], w_down[0], norm_final_g)
    rope = _rope_tables(max(x_prompt.shape[1], x_sample.shape[1]))
    return (_layer(x_prompt, mem_prompt, rope, *args), _layer(x_sample, mem_sample, rope, *args))
```

```python
import functools

import jax
import jax.numpy as jnp
from jax import lax
from jax.experimental import pallas as pl
from jax.experimental.pallas import tpu as pltpu

HEAD_DIM = 128
A_HEADS, A_KV_HEADS = 8, 2
B_HEADS, B_KV_HEADS = 4, 2
M_HEADS = 4
WINDOW = 128
GRID_W = 64
ROPE_THETA = 10000.0
NORM_EPS = 1e-6
NEG_INF = -1e30
A_Q, A_KV = A_HEADS * HEAD_DIM, A_KV_HEADS * HEAD_DIM
B_Q, B_KV = B_HEADS * HEAD_DIM, B_KV_HEADS * HEAD_DIM
M_Q = M_HEADS * HEAD_DIM
LOG2E = 1.4426950408889634
Q_SCALE = HEAD_DIM ** -0.5 * LOG2E
SOFTMAX_BOUND_LIMIT = 32.0
ROUNDING_SLACK = 1.02

BF16 = jnp.bfloat16
F32 = jnp.float32

VMEM_LIMIT_BYTES = 56 * 1024 * 1024


def _params(*sem):
    return pltpu.CompilerParams(dimension_semantics=sem, vmem_limit_bytes=VMEM_LIMIT_BYTES)


def _resident(shape, index_map):
    return pl.BlockSpec(shape, index_map, pipeline_mode=pl.Buffered(1))


def _rms(x, g):
    ms = jnp.mean(x * x, axis=-1, keepdims=True)
    return x * lax.rsqrt(ms + NORM_EPS) * g


def _dot(a, b):
    return jnp.dot(a, b, preferred_element_type=F32)


def _dot_nt(a, b):
    return lax.dot_general(a, b, (((1,), (1,)), ((), ())), preferred_element_type=F32)


def _inproj_kernel(x_ref, g_ref, w_ref, ca_ref, sa_ref, cb_ref, sb_ref, qg_ref, kg_ref,
                   qa_ref, ka_ref, va_ref, qb_ref, kb_ref, vbt_ref, qm_ref):
    h = _rms(x_ref[...], g_ref[...]).astype(BF16)
    ca, sa, cb, sb = ca_ref[...], sa_ref[...], cb_ref[...], sb_ref[...]
    lane = lax.broadcasted_iota(jnp.int32, ca.shape, 1)
    low_quarter = (lane % (HEAD_DIM // 2)) < (HEAD_DIM // 4)

    def rope_a(t):
        return t * ca + pltpu.roll(t, HEAD_DIM // 2, 1) * sa

    def rope_b(t):
        partner = jnp.where(low_quarter, pltpu.roll(t, HEAD_DIM - HEAD_DIM // 4, 1),
                            pltpu.roll(t, HEAD_DIM // 4, 1))
        return t * cb + partner * sb

    def heads(t, n, fn, out_ref):
        for i in range(n):
            sl = slice(i * HEAD_DIM, (i + 1) * HEAD_DIM)
            out_ref[:, sl] = fn(t[:, sl]).astype(out_ref.dtype)

    widths = (A_Q, A_KV, A_KV, B_Q, B_KV, B_KV, M_Q)
    starts = [sum(widths[:i]) for i in range(len(widths))]

    def section(i):
        return _dot(h, w_ref[:, starts[i]:starts[i] + widths[i]])

    qg, kg = qg_ref[...], kg_ref[...]
    heads(section(3), B_HEADS, lambda t: rope_b(_rms(t, qg)) * Q_SCALE, qb_ref)
    heads(section(4), B_KV_HEADS, lambda t: rope_b(_rms(t, kg)), kb_ref)
    vbt_ref[...] = section(5).astype(vbt_ref.dtype).T
    heads(section(0), A_HEADS, lambda t: rope_a(t) * Q_SCALE, qa_ref)
    heads(section(1), A_KV_HEADS, rope_a, ka_ref)
    qm_ref[...] = (section(6) * Q_SCALE).astype(qm_ref.dtype)
    va_ref[...] = section(2).astype(va_ref.dtype)


def _rope_tables(seq):
    def tables(n, dim):
        inv = ROPE_THETA ** (-(jnp.arange(0, dim, 2, dtype=F32) / dim))
        ang = jnp.arange(n, dtype=F32)[:, None] * jnp.tile(inv, 2)[None, :]
        sign = jnp.where(jnp.arange(dim) < dim // 2, -1.0, 1.0).astype(F32)
        return jnp.cos(ang), jnp.sin(ang) * sign

    ca, sa = tables(seq, HEAD_DIM)
    n_rows = seq // GRID_W
    half = HEAD_DIM // 2
    per_row = lambda t: jnp.broadcast_to(t[:, None, :], (n_rows, GRID_W, half))
    per_col = lambda t: jnp.broadcast_to(t[None, :, :], (n_rows, GRID_W, half))
    cr, sr = tables(n_rows, half)
    cc, sc = tables(GRID_W, half)
    cb = jnp.concatenate([per_row(cr), per_col(cc)], axis=-1).reshape(seq, HEAD_DIM)
    sb = jnp.concatenate([per_row(sr), per_col(sc)], axis=-1).reshape(seq, HEAD_DIM)
    return ca, sa, cb, sb


def _inproj(x2, seq, g, w_bf, qg, kg, rope, tm, kv_block):
    tokens, d = x2.shape
    per_seq = seq // tm
    per_kv = kv_block // tm
    assert kv_block % tm == 0 and seq % kv_block == 0
    ca, sa, cb, sb = rope
    row = lambda i: (i, 0)
    fixed = lambda i: (0, 0)
    tab = lambda i: (i % per_seq, 0)
    widths = (A_Q, A_KV, A_KV, B_Q, B_KV, B_KV, M_Q)
    out_specs = [pl.BlockSpec((tm, w), row) for w in widths]
    out_shape = [jax.ShapeDtypeStruct((tokens, w), BF16) for w in widths]
    out_specs[5] = pl.BlockSpec((None, B_KV, tm), lambda i: (i // per_kv, 0, i % per_kv))
    out_shape[5] = jax.ShapeDtypeStruct((tokens // kv_block, B_KV, kv_block), BF16)
    return pl.pallas_call(
        _inproj_kernel,
        grid=(tokens // tm,),
        in_specs=[pl.BlockSpec((tm, d), row),
                  _resident((1, d), fixed),
                  _resident(w_bf.shape, fixed),
                  pl.BlockSpec((tm, HEAD_DIM), tab), pl.BlockSpec((tm, HEAD_DIM), tab),
                  pl.BlockSpec((tm, HEAD_DIM), tab), pl.BlockSpec((tm, HEAD_DIM), tab),
                  _resident((1, HEAD_DIM), fixed), _resident((1, HEAD_DIM), fixed)],
        out_specs=out_specs,
        out_shape=out_shape,
        compiler_params=_params("parallel"),
        name="inproj",
    )(x2, g.reshape(1, d), w_bf, ca, sa, cb, sb, qg.reshape(1, HEAD_DIM), kg.reshape(1, HEAD_DIM))


def _memkv_kernel(mem_ref, g_ref, w_ref, km_ref, vm_ref):
    h = _rms(mem_ref[...], g_ref[...]).astype(BF16)
    kv = _dot(h, w_ref[...])
    km_ref[...] = kv[:, :M_Q].astype(km_ref.dtype)
    vm_ref[...] = kv[:, M_Q:].astype(vm_ref.dtype)


def _memkv(mem, g, w_bf):
    b, n, d = mem.shape
    blk = lambda i: (i, 0, 0)
    fixed = lambda i: (0, 0)
    return pl.pallas_call(
        _memkv_kernel,
        grid=(b,),
        in_specs=[pl.BlockSpec((None, n, d), blk), _resident((1, d), fixed), _resident(w_bf.shape, fixed)],
        out_specs=[pl.BlockSpec((None, n, M_Q), blk), pl.BlockSpec((None, n, M_Q), blk)],
        out_shape=[jax.ShapeDtypeStruct((b, n, M_Q), BF16)] * 2,
        compiler_params=_params("parallel"),
        name="memkv",
    )(mem, g.reshape(1, d), w_bf)


def _local_kernel(sink_ref, qa_ref, kp_ref, kc_ref, kn_ref, vp_ref, vc_ref, vn_ref,
                  qm_ref, km_ref, vm_ref, ga_ref, gm_ref, oa_ref, om_ref,
                  kbuf, vbuf, vmbuf, sbuf, smbuf, obuf, mbuf):
    bq = qa_ref.shape[0]
    nsub = bq // WINDOW
    i = pl.program_id(1)
    first = i == 0
    last = i == pl.num_programs(1) - 1
    group = A_HEADS // A_KV_HEADS
    cols = 3 * WINDOW
    two = 2 * HEAD_DIM

    kbuf[0:WINDOW] = kp_ref[...]
    kbuf[WINDOW:WINDOW + bq] = kc_ref[...]
    kbuf[WINDOW + bq:] = kn_ref[...]
    for g in range(A_KV_HEADS):
        sl = slice(g * HEAD_DIM, (g + 1) * HEAD_DIM)
        vbuf[0:WINDOW, g * two:g * two + HEAD_DIM] = vp_ref[:, sl]
        vbuf[WINDOW:WINDOW + bq, g * two:g * two + HEAD_DIM] = vc_ref[:, sl]
        vbuf[WINDOW + bq:, g * two:g * two + HEAD_DIM] = vn_ref[:, sl]
        vbuf[:, g * two + HEAD_DIM:(g + 1) * two] = jnp.ones((bq + 2 * WINDOW, HEAD_DIM), vbuf.dtype)
    for h in range(M_HEADS):
        vmbuf[:, h * two:h * two + HEAD_DIM] = vm_ref[:, h * HEAD_DIM:(h + 1) * HEAD_DIM]
        vmbuf[:, h * two + HEAD_DIM:(h + 1) * two] = jnp.ones((vm_ref.shape[0], HEAD_DIM), vmbuf.dtype)

    for j in range(nsub):
        for g in range(A_KV_HEADS):
            q = jnp.concatenate(
                [qa_ref[j * WINDOW:(j + 1) * WINDOW, (g * group + h) * HEAD_DIM:(g * group + h + 1) * HEAD_DIM]
                 for h in range(group)], axis=0)
            sbuf[j * A_KV_HEADS + g] = _dot_nt(q, kbuf[j * WINDOW:j * WINDOW + cols, g * HEAD_DIM:(g + 1) * HEAD_DIM])
    for h in range(M_HEADS):
        sl = slice(h * HEAD_DIM, (h + 1) * HEAD_DIM)
        smbuf[h] = _dot_nt(qm_ref[:, sl], km_ref[:, sl])

    qi = lax.broadcasted_iota(jnp.int32, (WINDOW, WINDOW), 0)
    kc = lax.broadcasted_iota(jnp.int32, (WINDOW, WINDOW), 1)
    neg = jnp.float32(NEG_INF)
    tile_heads = lambda t: jnp.concatenate([t] * group, axis=0)
    bias_left = jnp.where(kc >= qi, 0.0, neg)
    bias_right = jnp.where(kc <= qi, 0.0, neg)
    bias_left_first = tile_heads(jnp.where(first, neg, bias_left))
    bias_right_last = tile_heads(jnp.where(last, neg, bias_right))
    bias_left, bias_right = tile_heads(bias_left), tile_heads(bias_right)

    def softmax_pv(chunks, m, v):
        p = jnp.concatenate([jnp.exp2(c - m) for c in chunks], axis=1).astype(BF16)
        pv = _dot(p, v)
        return pv[:, :HEAD_DIM], pv[:, HEAD_DIM:]

    for j in range(nsub):
        for g in range(A_KV_HEADS):
            s = sbuf[j * A_KV_HEADS + g]
            chunks = [s[:, 0:WINDOW] + (bias_left_first if j == 0 else bias_left),
                      s[:, WINDOW:2 * WINDOW],
                      s[:, 2 * WINDOW:] + (bias_right_last if j == nsub - 1 else bias_right)]
            sink = jnp.concatenate(
                [jnp.full((WINDOW, HEAD_DIM), sink_ref[g * group + h] * LOG2E, F32) for h in range(group)], axis=0)
            m = jnp.maximum(jnp.max(functools.reduce(jnp.maximum, chunks), axis=-1, keepdims=True), sink)
            num, den = softmax_pv(chunks, m, vbuf[j * WINDOW:j * WINDOW + cols, g * two:(g + 1) * two])
            o = num * (1.0 / (den + jnp.exp2(sink - m)))
            for h in range(group):
                c0 = (g * group + h) * HEAD_DIM
                obuf[j * WINDOW:(j + 1) * WINDOW, c0:c0 + HEAD_DIM] = o[h * WINDOW:(h + 1) * WINDOW]
    oa_ref[...] = _rms(obuf[...], ga_ref[...]).astype(oa_ref.dtype)

    for h in range(M_HEADS):
        s = smbuf[h]
        chunks = [s[:, c * HEAD_DIM:(c + 1) * HEAD_DIM] for c in range(s.shape[1] // HEAD_DIM)]
        m = jnp.max(functools.reduce(jnp.maximum, chunks), axis=-1, keepdims=True)
        m = jnp.broadcast_to(m, chunks[0].shape)
        num, den = softmax_pv(chunks, m, vmbuf[:, h * two:(h + 1) * two])
        mbuf[:, h * HEAD_DIM:(h + 1) * HEAD_DIM] = num * (1.0 / den)
    om_ref[...] = _rms(mbuf[...], gm_ref[...]).astype(om_ref.dtype)


def _local_attention(qa, ka, va, qm, km, vm, sink, ga, gm, bq):
    b, seq, _ = qa.shape
    nt = seq // bq
    sub = bq // WINDOW
    nblk = seq // WINDOW
    n_mem = km.shape[1]
    tile = lambda bi, i: (bi, i, 0)
    prev = lambda bi, i: (bi, jnp.maximum(i * sub - 1, 0), 0)
    nxt = lambda bi, i: (bi, jnp.minimum((i + 1) * sub, nblk - 1), 0)
    per_b = lambda bi, i: (bi, 0, 0)
    fixed = lambda bi, i: (0, 0)
    kv_specs = [pl.BlockSpec((None, WINDOW, A_KV), prev), pl.BlockSpec((None, bq, A_KV), tile),
                pl.BlockSpec((None, WINDOW, A_KV), nxt)]
    return pl.pallas_call(
        _local_kernel,
        grid=(b, nt),
        in_specs=[pl.BlockSpec(memory_space=pltpu.SMEM),
                  pl.BlockSpec((None, bq, A_Q), tile)] + kv_specs + kv_specs + [
                  pl.BlockSpec((None, bq, M_Q), tile),
                  pl.BlockSpec((None, n_mem, M_Q), per_b), pl.BlockSpec((None, n_mem, M_Q), per_b),
                  _resident((1, A_Q), fixed), _resident((1, M_Q), fixed)],
        out_specs=[pl.BlockSpec((None, bq, A_Q), tile), pl.BlockSpec((None, bq, M_Q), tile)],
        out_shape=[jax.ShapeDtypeStruct((b, seq, A_Q), BF16), jax.ShapeDtypeStruct((b, seq, M_Q), BF16)],
        scratch_shapes=[pltpu.VMEM((bq + 2 * WINDOW, A_KV), BF16),
                        pltpu.VMEM((bq + 2 * WINDOW, 2 * A_KV), BF16),
                        pltpu.VMEM((n_mem, 2 * M_Q), BF16),
                        pltpu.VMEM((sub * A_KV_HEADS, (A_HEADS // A_KV_HEADS) * WINDOW, 3 * WINDOW), F32),
                        pltpu.VMEM((M_HEADS, bq, n_mem), F32),
                        pltpu.VMEM((bq, A_Q), F32), pltpu.VMEM((bq, M_Q), F32)],
        compiler_params=_params("parallel", "parallel"),
        name="local_attn",
    )(sink, qa, ka, ka, ka, va, va, va, qm, km, vm, ga.reshape(1, A_Q), gm.reshape(1, M_Q))


def _global_heads(q_ref, k_ref, vt_ref, obuf, bounded):
    bq = q_ref.shape[0]
    n_kv, _, bk = vt_ref.shape
    group = B_HEADS // B_KV_HEADS
    rows = group * bq
    sub = 8
    l0 = jnp.zeros((sub, rows), F32)
    acc0 = jnp.zeros((HEAD_DIM, rows), F32)

    for g in range(B_KV_HEADS):
        q = jnp.concatenate(
            [q_ref[:, (g * group + h) * HEAD_DIM:(g * group + h + 1) * HEAD_DIM] for h in range(group)], axis=0)
        ksl = slice(g * HEAD_DIM, (g + 1) * HEAD_DIM)

        def logits_t(j, q=q, ksl=ksl):
            start = pl.multiple_of(j * bk, bk)
            return _dot_nt(k_ref[pl.ds(start, bk), ksl], q).reshape(bk // sub, sub, rows)

        def pv_t(j, pt, ksl=ksl):
            return _dot(vt_ref[j, ksl, :], pt.reshape(bk, rows).astype(BF16))

        def bounded_body(j, carry, logits_t=logits_t, pv_t=pv_t):
            l, acc = carry
            pt = jnp.exp2(logits_t(j))
            return l + jnp.sum(pt, axis=0), acc + pv_t(j, pt)

        def online_body(j, carry, logits_t=logits_t, pv_t=pv_t):
            m, l, acc = carry
            st = logits_t(j)
            m_new = jnp.maximum(m, jnp.max(jnp.max(st, axis=0), axis=0, keepdims=True))
            alpha = jnp.exp2(m - m_new)
            pt = jnp.exp2(st - m_new[None])
            return m_new, alpha * l + jnp.sum(pt, axis=0), alpha[:1] * acc + pv_t(j, pt)

        if bounded:
            l, acc = lax.fori_loop(0, n_kv, bounded_body, (l0, acc0), unroll=True)
        else:
            _, l, acc = lax.fori_loop(0, n_kv, online_body, (jnp.full((sub, rows), NEG_INF, F32), l0, acc0))
        o = (acc * (1.0 / jnp.sum(l, axis=0, keepdims=True))).T
        for h in range(group):
            c0 = (g * group + h) * HEAD_DIM
            obuf[:, c0:c0 + HEAD_DIM] = o[h * bq:(h + 1) * bq]


def _global_kernel(bounded_ref, q_ref, k_ref, vt_ref, g_ref, o_ref, obuf):
    bounded = bounded_ref[0] == 1

    @pl.when(bounded)
    def _():
        _global_heads(q_ref, k_ref, vt_ref, obuf, True)

    @pl.when(jnp.logical_not(bounded))
    def _():
        _global_heads(q_ref, k_ref, vt_ref, obuf, False)

    o_ref[...] = _rms(obuf[...], g_ref[...]).astype(o_ref.dtype)


def _logits_bounded(qg, kg):
    bound = (HEAD_DIM ** 0.5) * jnp.max(jnp.abs(qg)) * jnp.max(jnp.abs(kg)) * ROUNDING_SLACK
    return (bound <= SOFTMAX_BOUND_LIMIT).astype(jnp.int32).reshape(1)


def _global_attention(q, k, vt, gb, bounded, bq):
    b, seq, _ = q.shape
    n_kv, _, bk = vt.shape[1:]
    tile = lambda bi, i: (bi, i, 0)
    fixed = lambda bi, i: (0, 0)
    return pl.pallas_call(
        _global_kernel,
        grid=(b, seq // bq),
        in_specs=[pl.BlockSpec(memory_space=pltpu.SMEM),
                  pl.BlockSpec((None, bq, B_Q), tile),
                  _resident((None, seq, B_KV), lambda bi, i: (bi, 0, 0)),
                  _resident((None, n_kv, B_KV, bk), lambda bi, i: (bi, 0, 0, 0)),
                  _resident((1, B_Q), fixed)],
        out_specs=pl.BlockSpec((None, bq, B_Q), tile),
        out_shape=jax.ShapeDtypeStruct((b, seq, B_Q), BF16),
        scratch_shapes=[pltpu.VMEM((bq, B_Q), F32)],
        compiler_params=_params("parallel", "parallel"),
        name="global_attn",
    )(bounded, q, k, vt, gb.reshape(1, B_Q))


def _outproj_kernel(oa_ref, ob_ref, om_ref, x_ref, w_ref, x1_ref):
    merged = jnp.concatenate([oa_ref[...], ob_ref[...], om_ref[...]], axis=1)
    x1_ref[...] = x_ref[...] + _dot(merged, w_ref[...])


def _outproj(oa, ob, om, x2, w_bf, tm):
    tokens, d = x2.shape
    row = lambda i: (i, 0)
    fixed = lambda i: (0, 0)
    return pl.pallas_call(
        _outproj_kernel,
        grid=(tokens // tm,),
        in_specs=[pl.BlockSpec((tm, A_Q), row), pl.BlockSpec((tm, B_Q), row), pl.BlockSpec((tm, M_Q), row),
                  pl.BlockSpec((tm, d), row), _resident(w_bf.shape, fixed)],
        out_specs=pl.BlockSpec((tm, d), row),
        out_shape=jax.ShapeDtypeStruct((tokens, d), F32),
        compiler_params=_params("parallel"),
        name="outproj",
    )(oa, ob, om, x2, w_bf)


def _ffn_kernel(x1_ref, gf_ref, wg_ref, wu_ref, wd_ref, gl_ref, o_ref, h_scr):
    j = pl.program_id(1)

    @pl.when(j == 0)
    def _():
        x1 = x1_ref[...]
        h_scr[...] = _rms(x1, gf_ref[...]).astype(h_scr.dtype)
        o_ref[...] = x1

    h = h_scr[...]
    gate = _dot(h, wg_ref[...])
    up = _dot(h, wu_ref[...])
    act = (gate * jax.nn.sigmoid(gate) * up).astype(BF16)
    o_ref[...] += _dot(act, wd_ref[...])

    @pl.when(j == pl.num_programs(1) - 1)
    def _():
        o_ref[...] = _rms(o_ref[...], gl_ref[...])


def _ffn(x1, gf, w_gu_bf, w_dn_bf, gl, tm, tf):
    tokens, d = x1.shape
    d_ff = w_dn_bf.shape[0]
    nf = d_ff // tf
    row = lambda i, j: (i, 0)
    fixed = lambda i, j: (0, 0)
    return pl.pallas_call(
        _ffn_kernel,
        grid=(tokens // tm, nf),
        in_specs=[pl.BlockSpec((tm, d), row), _resident((1, d), fixed),
                  pl.BlockSpec((d, tf), lambda i, j: (0, j)), pl.BlockSpec((d, tf), lambda i, j: (0, j + nf)),
                  pl.BlockSpec((tf, d), lambda i, j: (j, 0)), _resident((1, d), fixed)],
        out_specs=pl.BlockSpec((tm, d), row),
        out_shape=jax.ShapeDtypeStruct((tokens, d), F32),
        scratch_shapes=[pltpu.VMEM((tm, d), BF16)],
        compiler_params=_params("parallel", "arbitrary"),
        name="ffn",
    )(x1, gf.reshape(1, d), w_gu_bf, w_gu_bf, w_dn_bf, gl.reshape(1, d))


def _tile(n, pref):
    t = min(pref, n)
    assert n % t == 0, (n, t)
    return t


def _tiles(seq, d_ff):
    return dict(
        proj=_tile(seq, 512),
        local=_tile(seq, 512),
        global_q=_tile(seq, 512),
        global_kv=_tile(seq, 4096),
        ffn=_tile(seq, 1024),
        ffn_cols=_tile(d_ff, 512),
    )


def _layer(x, mem, rope, norm_mix_g, norm_mem_g, w_in, w_mem_kv, sink_a, q_norm_b_g, k_norm_b_g,
           out_norm_g, w_out, norm_ffn_g, w_gate_up, w_down, final_g):
    b, seq, d = x.shape
    tokens = b * seq
    assert seq % GRID_W == 0 and seq % WINDOW == 0
    x2 = x.reshape(tokens, d)
    t = _tiles(seq, w_down.shape[0])

    qa, ka, va, qb, kb, vbt, qm = _inproj(x2, seq, norm_mix_g, w_in.astype(BF16), q_norm_b_g, k_norm_b_g,
                                          rope, t["proj"], t["global_kv"])
    km, vm = _memkv(mem, norm_mem_g, w_mem_kv.astype(BF16))

    r3 = lambda a: a.reshape(b, seq, a.shape[-1])
    oa, om = _local_attention(r3(qa), r3(ka), r3(va), r3(qm), km, vm, sink_a,
                              out_norm_g[:A_Q], out_norm_g[A_Q + B_Q:], t["local"])
    vbt = vbt.reshape(b, seq // t["global_kv"], B_KV, t["global_kv"])
    ob = _global_attention(r3(qb), r3(kb), vbt, out_norm_g[A_Q:A_Q + B_Q],
                           _logits_bounded(q_norm_b_g, k_norm_b_g), t["global_q"])

    x1 = _outproj(oa.reshape(tokens, A_Q), ob.reshape(tokens, B_Q), om.reshape(tokens, M_Q), x2,
                  w_out.astype(BF16), t["proj"])
    y = _ffn(x1, norm_ffn_g, w_gate_up.astype(BF16), w_down.astype(BF16), final_g, t["ffn"], t["ffn_cols"])
    return y.reshape(b, seq, d)


def kernel(x_prompt, x_sample, mem_prompt, mem_sample, norm_mix_g, norm_mem_g, w_in, w_mem_kv, sink_a,
           q_norm_b_g, k_norm_b_g, out_norm_g, w_out, norm_ffn_g, w_gate_up, w_down, norm_final_g):
    depth = w_in.shape[0]
    assert depth == 1, "final norm is fused into the last layer's ffn kernel"
    args = (norm_mix_g[0], norm_mem_g[0], w_in[0], w_mem_kv[0], sink_a[0], q_norm_b_g[0], k_norm_b_g[0],
            out_norm_g[0], w_out[0], norm_ffn_g[0], w_gate_up[0], w_down[0], norm_final_g)
    rope = _rope_tables(max(x_prompt.shape[1], x_sample.shape[1]))
    return (_layer(x_prompt, mem_prompt, rope, *args), _layer(x_sample, mem_sample, rope, *args))
```

```python
import functools

import jax
import jax.numpy as jnp
from jax import lax
from jax.experimental import pallas as pl
from jax.experimental.pallas import tpu as pltpu

HEAD_DIM = 128
A_HEADS, A_KV_HEADS = 8, 2
B_HEADS, B_KV_HEADS = 4, 2
M_HEADS = 4
WINDOW = 128
GRID_W = 64
ROPE_THETA = 10000.0
NORM_EPS = 1e-6
NEG_INF = -1e30
A_Q, A_KV = A_HEADS * HEAD_DIM, A_KV_HEADS * HEAD_DIM
B_Q, B_KV = B_HEADS * HEAD_DIM, B_KV_HEADS * HEAD_DIM
M_Q = M_HEADS * HEAD_DIM
LOG2E = 1.4426950408889634
Q_SCALE = HEAD_DIM ** -0.5 * LOG2E
SOFTMAX_BOUND_LIMIT = 32.0
ROUNDING_SLACK = 1.02
KV_UNROLL = 2

BF16 = jnp.bfloat16
F32 = jnp.float32
F32_SUBLANES = 8

VMEM_LIMIT_BYTES = 56 * 1024 * 1024


def _params(*sem):
    return pltpu.CompilerParams(dimension_semantics=sem, vmem_limit_bytes=VMEM_LIMIT_BYTES)


def _resident(shape, index_map):
    return pl.BlockSpec(shape, index_map, pipeline_mode=pl.Buffered(1))


def _rms(x, g):
    ms = jnp.mean(x * x, axis=-1, keepdims=True)
    return x * lax.rsqrt(ms + NORM_EPS) * g


def _dot(a, b):
    return jnp.dot(a, b, preferred_element_type=F32)


def _dot_nt(a, b):
    return lax.dot_general(a, b, (((1,), (1,)), ((), ())), preferred_element_type=F32)


def _inproj_kernel(x_ref, g_ref, w_ref, ca_ref, sa_ref, cb_ref, sb_ref, qg_ref, kg_ref,
                   qa_ref, ka_ref, va_ref, qb_ref, kb_ref, vbt_ref, qm_ref):
    h = _rms(x_ref[...], g_ref[...]).astype(BF16)
    ca, sa, cb, sb = ca_ref[...], sa_ref[...], cb_ref[...], sb_ref[...]
    lane = lax.broadcasted_iota(jnp.int32, ca.shape, 1)
    low_quarter = (lane % (HEAD_DIM // 2)) < (HEAD_DIM // 4)

    def rope_a(t):
        return t * ca + pltpu.roll(t, HEAD_DIM // 2, 1) * sa

    def rope_b(t):
        partner = jnp.where(low_quarter, pltpu.roll(t, HEAD_DIM - HEAD_DIM // 4, 1),
                            pltpu.roll(t, HEAD_DIM // 4, 1))
        return t * cb + partner * sb

    def heads(t, n, fn, out_ref):
        for i in range(n):
            sl = slice(i * HEAD_DIM, (i + 1) * HEAD_DIM)
            out_ref[:, sl] = fn(t[:, sl]).astype(out_ref.dtype)

    widths = (A_Q, A_KV, A_KV, B_Q, B_KV, B_KV, M_Q)
    starts = [sum(widths[:i]) for i in range(len(widths))]

    def section(i):
        return _dot(h, w_ref[:, starts[i]:starts[i] + widths[i]])

    qg, kg = qg_ref[...], kg_ref[...]
    heads(section(3), B_HEADS, lambda t: rope_b(_rms(t, qg)) * Q_SCALE, qb_ref)
    heads(section(4), B_KV_HEADS, lambda t: rope_b(_rms(t, kg)), kb_ref)
    vbt_ref[...] = section(5).astype(vbt_ref.dtype).T
    heads(section(0), A_HEADS, lambda t: rope_a(t) * Q_SCALE, qa_ref)
    heads(section(1), A_KV_HEADS, rope_a, ka_ref)
    qm_ref[...] = (section(6) * Q_SCALE).astype(qm_ref.dtype)
    va_ref[...] = section(2).astype(va_ref.dtype)


def _rope_tables(seq):
    def tables(n, dim):
        inv = ROPE_THETA ** (-(jnp.arange(0, dim, 2, dtype=F32) / dim))
        ang = jnp.arange(n, dtype=F32)[:, None] * jnp.tile(inv, 2)[None, :]
        sign = jnp.where(jnp.arange(dim) < dim // 2, -1.0, 1.0).astype(F32)
        return jnp.cos(ang), jnp.sin(ang) * sign

    ca, sa = tables(seq, HEAD_DIM)
    n_rows = seq // GRID_W
    half = HEAD_DIM // 2
    per_row = lambda t: jnp.broadcast_to(t[:, None, :], (n_rows, GRID_W, half))
    per_col = lambda t: jnp.broadcast_to(t[None, :, :], (n_rows, GRID_W, half))
    cr, sr = tables(n_rows, half)
    cc, sc = tables(GRID_W, half)
    cb = jnp.concatenate([per_row(cr), per_col(cc)], axis=-1).reshape(seq, HEAD_DIM)
    sb = jnp.concatenate([per_row(sr), per_col(sc)], axis=-1).reshape(seq, HEAD_DIM)
    return ca, sa, cb, sb


def _inproj(x2, seq, g, w_bf, qg, kg, rope, tm, kv_block):
    tokens, d = x2.shape
    per_seq = seq // tm
    per_kv = kv_block // tm
    assert kv_block % tm == 0 and seq % kv_block == 0
    ca, sa, cb, sb = rope
    row = lambda i: (i, 0)
    fixed = lambda i: (0, 0)
    tab = lambda i: (i % per_seq, 0)
    widths = (A_Q, A_KV, A_KV, B_Q, B_KV, B_KV, M_Q)
    out_specs = [pl.BlockSpec((tm, w), row) for w in widths]
    out_shape = [jax.ShapeDtypeStruct((tokens, w), BF16) for w in widths]
    out_specs[5] = pl.BlockSpec((None, B_KV, tm), lambda i: (i // per_kv, 0, i % per_kv))
    out_shape[5] = jax.ShapeDtypeStruct((tokens // kv_block, B_KV, kv_block), BF16)
    return pl.pallas_call(
        _inproj_kernel,
        grid=(tokens // tm,),
        in_specs=[pl.BlockSpec((tm, d), row),
                  _resident((1, d), fixed),
                  _resident(w_bf.shape, fixed),
                  pl.BlockSpec((tm, HEAD_DIM), tab), pl.BlockSpec((tm, HEAD_DIM), tab),
                  pl.BlockSpec((tm, HEAD_DIM), tab), pl.BlockSpec((tm, HEAD_DIM), tab),
                  _resident((1, HEAD_DIM), fixed), _resident((1, HEAD_DIM), fixed)],
        out_specs=out_specs,
        out_shape=out_shape,
        compiler_params=_params("parallel"),
        name="inproj",
    )(x2, g.reshape(1, d), w_bf, ca, sa, cb, sb, qg.reshape(1, HEAD_DIM), kg.reshape(1, HEAD_DIM))


def _memkv_kernel(mem_ref, g_ref, w_ref, km_ref, vm_ref):
    h = _rms(mem_ref[...], g_ref[...]).astype(BF16)
    kv = _dot(h, w_ref[...])
    km_ref[...] = kv[:, :M_Q].astype(km_ref.dtype)
    vm_ref[...] = kv[:, M_Q:].astype(vm_ref.dtype)


def _memkv(mem, g, w_bf):
    b, n, d = mem.shape
    blk = lambda i: (i, 0, 0)
    fixed = lambda i: (0, 0)
    return pl.pallas_call(
        _memkv_kernel,
        grid=(b,),
        in_specs=[pl.BlockSpec((None, n, d), blk), _resident((1, d), fixed), _resident(w_bf.shape, fixed)],
        out_specs=[pl.BlockSpec((None, n, M_Q), blk), pl.BlockSpec((None, n, M_Q), blk)],
        out_shape=[jax.ShapeDtypeStruct((b, n, M_Q), BF16)] * 2,
        compiler_params=_params("parallel"),
        name="memkv",
    )(mem, g.reshape(1, d), w_bf)


def _local_kernel(sink_ref, qa_ref, kp_ref, kc_ref, kn_ref, vp_ref, vc_ref, vn_ref,
                  qm_ref, km_ref, vm_ref, ga_ref, gm_ref, oa_ref, om_ref,
                  kbuf, vbuf, vmbuf, sbuf, smbuf, obuf, mbuf):
    bq = qa_ref.shape[0]
    nsub = bq // WINDOW
    i = pl.program_id(1)
    first = i == 0
    last = i == pl.num_programs(1) - 1
    group = A_HEADS // A_KV_HEADS
    cols = 3 * WINDOW
    two = 2 * HEAD_DIM

    kbuf[0:WINDOW] = kp_ref[...]
    kbuf[WINDOW:WINDOW + bq] = kc_ref[...]
    kbuf[WINDOW + bq:] = kn_ref[...]
    for g in range(A_KV_HEADS):
        sl = slice(g * HEAD_DIM, (g + 1) * HEAD_DIM)
        vbuf[0:WINDOW, g * two:g * two + HEAD_DIM] = vp_ref[:, sl]
        vbuf[WINDOW:WINDOW + bq, g * two:g * two + HEAD_DIM] = vc_ref[:, sl]
        vbuf[WINDOW + bq:, g * two:g * two + HEAD_DIM] = vn_ref[:, sl]
        vbuf[:, g * two + HEAD_DIM:(g + 1) * two] = jnp.ones((bq + 2 * WINDOW, HEAD_DIM), vbuf.dtype)
    for h in range(M_HEADS):
        vmbuf[:, h * two:h * two + HEAD_DIM] = vm_ref[:, h * HEAD_DIM:(h + 1) * HEAD_DIM]
        vmbuf[:, h * two + HEAD_DIM:(h + 1) * two] = jnp.ones((vm_ref.shape[0], HEAD_DIM), vmbuf.dtype)

    for j in range(nsub):
        for g in range(A_KV_HEADS):
            q = jnp.concatenate(
                [qa_ref[j * WINDOW:(j + 1) * WINDOW, (g * group + h) * HEAD_DIM:(g * group + h + 1) * HEAD_DIM]
                 for h in range(group)], axis=0)
            sbuf[j * A_KV_HEADS + g] = _dot_nt(q, kbuf[j * WINDOW:j * WINDOW + cols, g * HEAD_DIM:(g + 1) * HEAD_DIM])
    for h in range(M_HEADS):
        sl = slice(h * HEAD_DIM, (h + 1) * HEAD_DIM)
        smbuf[h] = _dot_nt(qm_ref[:, sl], km_ref[:, sl])

    qi = lax.broadcasted_iota(jnp.int32, (WINDOW, WINDOW), 0)
    kc = lax.broadcasted_iota(jnp.int32, (WINDOW, WINDOW), 1)
    neg = jnp.float32(NEG_INF)
    tile_heads = lambda t: jnp.concatenate([t] * group, axis=0)
    bias_left = jnp.where(kc >= qi, 0.0, neg)
    bias_right = jnp.where(kc <= qi, 0.0, neg)
    bias_left_first = tile_heads(jnp.where(first, neg, bias_left))
    bias_right_last = tile_heads(jnp.where(last, neg, bias_right))
    bias_left, bias_right = tile_heads(bias_left), tile_heads(bias_right)

    def softmax_pv(chunks, m, v):
        p = jnp.concatenate([jnp.exp2(c - m) for c in chunks], axis=1).astype(BF16)
        pv = _dot(p, v)
        return pv[:, :HEAD_DIM], pv[:, HEAD_DIM:]

    for j in range(nsub):
        for g in range(A_KV_HEADS):
            s = sbuf[j * A_KV_HEADS + g]
            chunks = [s[:, 0:WINDOW] + (bias_left_first if j == 0 else bias_left),
                      s[:, WINDOW:2 * WINDOW],
                      s[:, 2 * WINDOW:] + (bias_right_last if j == nsub - 1 else bias_right)]
            sink = jnp.concatenate(
                [jnp.full((WINDOW, HEAD_DIM), sink_ref[g * group + h] * LOG2E, F32) for h in range(group)], axis=0)
            m = jnp.maximum(jnp.max(functools.reduce(jnp.maximum, chunks), axis=-1, keepdims=True), sink)
            num, den = softmax_pv(chunks, m, vbuf[j * WINDOW:j * WINDOW + cols, g * two:(g + 1) * two])
            o = num * (1.0 / (den + jnp.exp2(sink - m)))
            for h in range(group):
                c0 = (g * group + h) * HEAD_DIM
                obuf[j * WINDOW:(j + 1) * WINDOW, c0:c0 + HEAD_DIM] = o[h * WINDOW:(h + 1) * WINDOW]
    oa_ref[...] = _rms(obuf[...], ga_ref[...]).astype(oa_ref.dtype)

    for h in range(M_HEADS):
        s = smbuf[h]
        chunks = [s[:, c * HEAD_DIM:(c + 1) * HEAD_DIM] for c in range(s.shape[1] // HEAD_DIM)]
        m = jnp.max(functools.reduce(jnp.maximum, chunks), axis=-1, keepdims=True)
        m = jnp.broadcast_to(m, chunks[0].shape)
        num, den = softmax_pv(chunks, m, vmbuf[:, h * two:(h + 1) * two])
        mbuf[:, h * HEAD_DIM:(h + 1) * HEAD_DIM] = num * (1.0 / den)
    om_ref[...] = _rms(mbuf[...], gm_ref[...]).astype(om_ref.dtype)


def _local_attention(qa, ka, va, qm, km, vm, sink, ga, gm, bq):
    b, seq, _ = qa.shape
    nt = seq // bq
    sub = bq // WINDOW
    nblk = seq // WINDOW
    n_mem = km.shape[1]
    tile = lambda bi, i: (bi, i, 0)
    prev = lambda bi, i: (bi, jnp.maximum(i * sub - 1, 0), 0)
    nxt = lambda bi, i: (bi, jnp.minimum((i + 1) * sub, nblk - 1), 0)
    per_b = lambda bi, i: (bi, 0, 0)
    fixed = lambda bi, i: (0, 0)
    kv_specs = [pl.BlockSpec((None, WINDOW, A_KV), prev), pl.BlockSpec((None, bq, A_KV), tile),
                pl.BlockSpec((None, WINDOW, A_KV), nxt)]
    return pl.pallas_call(
        _local_kernel,
        grid=(b, nt),
        in_specs=[pl.BlockSpec(memory_space=pltpu.SMEM),
                  pl.BlockSpec((None, bq, A_Q), tile)] + kv_specs + kv_specs + [
                  pl.BlockSpec((None, bq, M_Q), tile),
                  pl.BlockSpec((None, n_mem, M_Q), per_b), pl.BlockSpec((None, n_mem, M_Q), per_b),
                  _resident((1, A_Q), fixed), _resident((1, M_Q), fixed)],
        out_specs=[pl.BlockSpec((None, bq, A_Q), tile), pl.BlockSpec((None, bq, M_Q), tile)],
        out_shape=[jax.ShapeDtypeStruct((b, seq, A_Q), BF16), jax.ShapeDtypeStruct((b, seq, M_Q), BF16)],
        scratch_shapes=[pltpu.VMEM((bq + 2 * WINDOW, A_KV), BF16),
                        pltpu.VMEM((bq + 2 * WINDOW, 2 * A_KV), BF16),
                        pltpu.VMEM((n_mem, 2 * M_Q), BF16),
                        pltpu.VMEM((sub * A_KV_HEADS, (A_HEADS // A_KV_HEADS) * WINDOW, 3 * WINDOW), F32),
                        pltpu.VMEM((M_HEADS, bq, n_mem), F32),
                        pltpu.VMEM((bq, A_Q), F32), pltpu.VMEM((bq, M_Q), F32)],
        compiler_params=_params("parallel", "parallel"),
        name="local_attn",
    )(sink, qa, ka, ka, ka, va, va, va, qm, km, vm, ga.reshape(1, A_Q), gm.reshape(1, M_Q))


def _global_heads(q_ref, k_ref, vt_ref, obuf, bounded):
    bq = q_ref.shape[0]
    n_kv, _, bk = vt_ref.shape
    group = B_HEADS // B_KV_HEADS
    rows = group * bq
    sub = F32_SUBLANES
    l0 = jnp.zeros((sub, rows), F32)
    acc0 = jnp.zeros((HEAD_DIM, rows), F32)
    kv_heads = range(B_KV_HEADS)
    ksl = [slice(g * HEAD_DIM, (g + 1) * HEAD_DIM) for g in kv_heads]
    qs = [jnp.concatenate([q_ref[:, (g * group + h) * HEAD_DIM:(g * group + h + 1) * HEAD_DIM]
                           for h in range(group)], axis=0) for g in kv_heads]

    def logits_t(g, j):
        start = pl.multiple_of(j * bk, bk)
        return _dot_nt(k_ref[pl.ds(start, bk), ksl[g]], qs[g]).reshape(bk // sub, sub, rows)

    def pv_t(g, j, pt):
        return _dot(vt_ref[j, ksl[g], :], pt.reshape(bk, rows).astype(BF16))

    def bounded_body(j, carry):
        out = []
        for g, (l, acc) in zip(kv_heads, carry):
            pt = jnp.exp2(logits_t(g, j))
            out.append((l + jnp.sum(pt, axis=0), acc + pv_t(g, j, pt)))
        return tuple(out)

    def online_body(j, carry):
        out = []
        for g, (m, l, acc) in zip(kv_heads, carry):
            st = logits_t(g, j)
            m_new = jnp.maximum(m, jnp.max(jnp.max(st, axis=0), axis=0, keepdims=True))
            alpha = jnp.exp2(m - m_new)
            pt = jnp.exp2(st - m_new[None])
            out.append((m_new, alpha * l + jnp.sum(pt, axis=0), alpha[:1] * acc + pv_t(g, j, pt)))
        return tuple(out)

    if bounded:
        res = lax.fori_loop(0, n_kv, bounded_body, ((l0, acc0),) * B_KV_HEADS, unroll=min(n_kv, KV_UNROLL))
    else:
        m0 = jnp.full((sub, rows), NEG_INF, F32)
        res = [r[1:] for r in lax.fori_loop(0, n_kv, online_body, ((m0, l0, acc0),) * B_KV_HEADS)]
    for g, (l, acc) in zip(kv_heads, res):
        o = (acc * (1.0 / jnp.sum(l, axis=0, keepdims=True))).T
        for h in range(group):
            c0 = (g * group + h) * HEAD_DIM
            obuf[:, c0:c0 + HEAD_DIM] = o[h * bq:(h + 1) * bq]


def _global_kernel(q_ref, k_ref, vt_ref, g_ref, o_ref, obuf, *, bounded):
    _global_heads(q_ref, k_ref, vt_ref, obuf, bounded)
    o_ref[...] = _rms(obuf[...], g_ref[...]).astype(o_ref.dtype)


def _logits_bounded(qg, kg):
    bound = (HEAD_DIM ** 0.5) * jnp.max(jnp.abs(qg)) * jnp.max(jnp.abs(kg)) * ROUNDING_SLACK
    return bound <= SOFTMAX_BOUND_LIMIT


def _global_attention(q, k, vt, gb, bounded, bq):
    b, seq, _ = q.shape
    n_kv, _, bk = vt.shape[1:]
    tile = lambda bi, i: (bi, i, 0)
    fixed = lambda bi, i: (0, 0)

    def call(is_bounded):
        return pl.pallas_call(
            functools.partial(_global_kernel, bounded=is_bounded),
            grid=(b, seq // bq),
            in_specs=[pl.BlockSpec((None, bq, B_Q), tile),
                      _resident((None, seq, B_KV), lambda bi, i: (bi, 0, 0)),
                      _resident((None, n_kv, B_KV, bk), lambda bi, i: (bi, 0, 0, 0)),
                      _resident((1, B_Q), fixed)],
            out_specs=pl.BlockSpec((None, bq, B_Q), tile),
            out_shape=jax.ShapeDtypeStruct((b, seq, B_Q), BF16),
            scratch_shapes=[pltpu.VMEM((bq, B_Q), F32)],
            compiler_params=_params("parallel", "parallel"),
            name="global_attn_bounded" if is_bounded else "global_attn_online",
        )

    return lax.cond(bounded, call(True), call(False), q, k, vt, gb.reshape(1, B_Q))


def _outproj_kernel(oa_ref, ob_ref, om_ref, x_ref, w_ref, x1_ref):
    merged = jnp.concatenate([oa_ref[...], ob_ref[...], om_ref[...]], axis=1)
    x1_ref[...] = x_ref[...] + _dot(merged, w_ref[...])


def _outproj(oa, ob, om, x2, w_bf, tm):
    tokens, d = x2.shape
    row = lambda i: (i, 0)
    fixed = lambda i: (0, 0)
    return pl.pallas_call(
        _outproj_kernel,
        grid=(tokens // tm,),
        in_specs=[pl.BlockSpec((tm, A_Q), row), pl.BlockSpec((tm, B_Q), row), pl.BlockSpec((tm, M_Q), row),
                  pl.BlockSpec((tm, d), row), _resident(w_bf.shape, fixed)],
        out_specs=pl.BlockSpec((tm, d), row),
        out_shape=jax.ShapeDtypeStruct((tokens, d), F32),
        compiler_params=_params("parallel"),
        name="outproj",
    )(oa, ob, om, x2, w_bf)


def _ffn_kernel(x1_ref, gf_ref, wg_ref, wu_ref, wd_ref, gl_ref, o_ref, h_scr):
    j = pl.program_id(1)

    @pl.when(j == 0)
    def _():
        x1 = x1_ref[...]
        h_scr[...] = _rms(x1, gf_ref[...]).astype(h_scr.dtype)
        o_ref[...] = x1

    h = h_scr[...]
    gate = _dot(h, wg_ref[...])
    up = _dot(h, wu_ref[...])
    act = (gate * jax.nn.sigmoid(gate) * up).astype(BF16)
    o_ref[...] += _dot(act, wd_ref[...])

    @pl.when(j == pl.num_programs(1) - 1)
    def _():
        o_ref[...] = _rms(o_ref[...], gl_ref[...])


def _ffn(x1, gf, w_gu_bf, w_dn_bf, gl, tm, tf):
    tokens, d = x1.shape
    d_ff = w_dn_bf.shape[0]
    nf = d_ff // tf
    row = lambda i, j: (i, 0)
    fixed = lambda i, j: (0, 0)
    return pl.pallas_call(
        _ffn_kernel,
        grid=(tokens // tm, nf),
        in_specs=[pl.BlockSpec((tm, d), row), _resident((1, d), fixed),
                  pl.BlockSpec((d, tf), lambda i, j: (0, j)), pl.BlockSpec((d, tf), lambda i, j: (0, j + nf)),
                  pl.BlockSpec((tf, d), lambda i, j: (j, 0)), _resident((1, d), fixed)],
        out_specs=pl.BlockSpec((tm, d), row),
        out_shape=jax.ShapeDtypeStruct((tokens, d), F32),
        scratch_shapes=[pltpu.VMEM((tm, d), BF16)],
        compiler_params=_params("parallel", "arbitrary"),
        name="ffn",
    )(x1, gf.reshape(1, d), w_gu_bf, w_gu_bf, w_dn_bf, gl.reshape(1, d))


def _tile(n, pref):
    t = min(pref, n)
    assert n % t == 0, (n, t)
    return t


def _tiles(seq, d_ff):
    return dict(
        proj=_tile(seq, 512),
        local=_tile(seq, 512),
        global_q=_tile(seq, 512),
        global_kv=_tile(seq, 4096),
        ffn=_tile(seq, 1024),
        ffn_cols=_tile(d_ff, 512),
    )


def _layer(x, mem, rope, norm_mix_g, norm_mem_g, w_in, w_mem_kv, sink_a, q_norm_b_g, k_norm_b_g,
           out_norm_g, w_out, norm_ffn_g, w_gate_up, w_down, final_g):
    b, seq, d = x.shape
    tokens = b * seq
    assert seq % GRID_W == 0 and seq % WINDOW == 0
    x2 = x.reshape(tokens, d)
    t = _tiles(seq, w_down.shape[0])

    qa, ka, va, qb, kb, vbt, qm = _inproj(x2, seq, norm_mix_g, w_in.astype(BF16), q_norm_b_g, k_norm_b_g,
                                          rope, t["proj"], t["global_kv"])
    km, vm = _memkv(mem, norm_mem_g, w_mem_kv.astype(BF16))

    r3 = lambda a: a.reshape(b, seq, a.shape[-1])
    oa, om = _local_attention(r3(qa), r3(ka), r3(va), r3(qm), km, vm, sink_a,
                              out_norm_g[:A_Q], out_norm_g[A_Q + B_Q:], t["local"])
    vbt = vbt.reshape(b, seq // t["global_kv"], B_KV, t["global_kv"])
    ob = _global_attention(r3(qb), r3(kb), vbt, out_norm_g[A_Q:A_Q + B_Q],
                           _logits_bounded(q_norm_b_g, k_norm_b_g), t["global_q"])

    x1 = _outproj(oa.reshape(tokens, A_Q), ob.reshape(tokens, B_Q), om.reshape(tokens, M_Q), x2,
                  w_out.astype(BF16), t["proj"])
    y = _ffn(x1, norm_ffn_g, w_gate_up.astype(BF16), w_down.astype(BF16), final_g, t["ffn"], t["ffn_cols"])
    return y.reshape(b, seq, d)


def kernel(x_prompt, x_sample, mem_prompt, mem_sample, norm_mix_g, norm_mem_g, w_in, w_mem_kv, sink_a,
           q_norm_b_g, k_norm_b_g, out_norm_g, w_out, norm_ffn_g, w_gate_up, w_down, norm_final_g):
    depth = w_in.shape[0]
    assert depth == 1, "final norm is fused into the last layer's ffn kernel"
    args = (norm_mix_g[0], norm_mem_g[0], w_in[0], w_mem_kv[0], sink_a[0], q_norm_b_g[0], k_norm_b_g[0],
            out_norm_g[0], w_out[0], norm_ffn_g[0], w_gate_up[0], w_down[0], norm_final_g)
    rope = _rope_tables(max(x_prompt.shape[1], x_sample.shape[1]))
    return (_layer(x_prompt, mem_prompt, rope, *args), _layer(x_sample, mem_sample, rope, *args))
```

```python
import functools

import jax
import jax.numpy as jnp
from jax import lax
from jax.experimental import pallas as pl
from jax.experimental.pallas import tpu as pltpu

HEAD_DIM = 128
A_HEADS, A_KV_HEADS = 8, 2
B_HEADS, B_KV_HEADS = 4, 2
M_HEADS = 4
WINDOW = 128
GRID_W = 64
ROPE_THETA = 10000.0
NORM_EPS = 1e-6
NEG_INF = -1e30
A_Q, A_KV = A_HEADS * HEAD_DIM, A_KV_HEADS * HEAD_DIM
B_Q, B_KV = B_HEADS * HEAD_DIM, B_KV_HEADS * HEAD_DIM
M_Q = M_HEADS * HEAD_DIM
LOG2E = 1.4426950408889634
Q_SCALE = HEAD_DIM ** -0.5 * LOG2E
SOFTMAX_BOUND_LIMIT = 32.0
ROUNDING_SLACK = 1.02
KV_UNROLL = 2
ROW_BLOCK = 256

BF16 = jnp.bfloat16
F32 = jnp.float32
F32_SUBLANES = 8

VMEM_LIMIT_BYTES = 56 * 1024 * 1024


def _params(*sem):
    return pltpu.CompilerParams(dimension_semantics=sem, vmem_limit_bytes=VMEM_LIMIT_BYTES)


def _resident(shape, index_map):
    return pl.BlockSpec(shape, index_map, pipeline_mode=pl.Buffered(1))


def _rms(x, g):
    ms = jnp.mean(x * x, axis=-1, keepdims=True)
    return x * lax.rsqrt(ms + NORM_EPS) * g


def _dot(a, b):
    return jnp.dot(a, b, preferred_element_type=F32)


def _dot_nt(a, b):
    return lax.dot_general(a, b, (((1,), (1,)), ((), ())), preferred_element_type=F32)


def _inproj_kernel(x_ref, g_ref, w_ref, ca_ref, sa_ref, cb_ref, sb_ref, qg_ref, kg_ref,
                   qa_ref, ka_ref, va_ref, qb_ref, kb_ref, vbt_ref, qm_ref):
    widths = (A_Q, A_KV, A_KV, B_Q, B_KV, B_KV, M_Q)
    starts = [sum(widths[:i]) for i in range(len(widths))]
    lane = lax.broadcasted_iota(jnp.int32, (ROW_BLOCK, HEAD_DIM), 1)
    low_quarter = (lane % (HEAD_DIM // 2)) < (HEAD_DIM // 4)
    qg, kg = qg_ref[...], kg_ref[...]
    blocks = [slice(r * ROW_BLOCK, (r + 1) * ROW_BLOCK) for r in range(x_ref.shape[0] // ROW_BLOCK)]
    hs = [_rms(x_ref[b], g_ref[...]).astype(BF16) for b in blocks]

    for b, h in zip(blocks, hs):
        ca, sa, cb, sb = ca_ref[b], sa_ref[b], cb_ref[b], sb_ref[b]

        def rope_a(t):
            return t * ca + pltpu.roll(t, HEAD_DIM // 2, 1) * sa

        def rope_b(t):
            partner = jnp.where(low_quarter, pltpu.roll(t, HEAD_DIM - HEAD_DIM // 4, 1),
                                pltpu.roll(t, HEAD_DIM // 4, 1))
            return t * cb + partner * sb

        def heads(t, n, fn, out_ref):
            for i in range(n):
                sl = slice(i * HEAD_DIM, (i + 1) * HEAD_DIM)
                out_ref[b, sl] = fn(t[:, sl]).astype(out_ref.dtype)

        def section(i):
            return _dot(h, w_ref[:, starts[i]:starts[i] + widths[i]])

        heads(section(3), B_HEADS, lambda t: rope_b(_rms(t, qg)) * Q_SCALE, qb_ref)
        heads(section(4), B_KV_HEADS, lambda t: rope_b(_rms(t, kg)), kb_ref)
        vbt_ref[:, b] = section(5).astype(vbt_ref.dtype).T
        heads(section(0), A_HEADS, lambda t: rope_a(t) * Q_SCALE, qa_ref)
        heads(section(1), A_KV_HEADS, rope_a, ka_ref)
        qm_ref[b, :] = (section(6) * Q_SCALE).astype(qm_ref.dtype)
        va_ref[b, :] = section(2).astype(va_ref.dtype)


def _rope_tables(seq):
    def tables(n, dim):
        inv = ROPE_THETA ** (-(jnp.arange(0, dim, 2, dtype=F32) / dim))
        ang = jnp.arange(n, dtype=F32)[:, None] * jnp.tile(inv, 2)[None, :]
        sign = jnp.where(jnp.arange(dim) < dim // 2, -1.0, 1.0).astype(F32)
        return jnp.cos(ang), jnp.sin(ang) * sign

    ca, sa = tables(seq, HEAD_DIM)
    n_rows = seq // GRID_W
    half = HEAD_DIM // 2
    per_row = lambda t: jnp.broadcast_to(t[:, None, :], (n_rows, GRID_W, half))
    per_col = lambda t: jnp.broadcast_to(t[None, :, :], (n_rows, GRID_W, half))
    cr, sr = tables(n_rows, half)
    cc, sc = tables(GRID_W, half)
    cb = jnp.concatenate([per_row(cr), per_col(cc)], axis=-1).reshape(seq, HEAD_DIM)
    sb = jnp.concatenate([per_row(sr), per_col(sc)], axis=-1).reshape(seq, HEAD_DIM)
    return ca, sa, cb, sb


def _inproj(x2, seq, g, w_bf, qg, kg, rope, tm, kv_block):
    tokens, d = x2.shape
    per_seq = seq // tm
    per_kv = kv_block // tm
    assert kv_block % tm == 0 and seq % kv_block == 0
    ca, sa, cb, sb = rope
    row = lambda i: (i, 0)
    fixed = lambda i: (0, 0)
    tab = lambda i: (i % per_seq, 0)
    widths = (A_Q, A_KV, A_KV, B_Q, B_KV, B_KV, M_Q)
    out_specs = [pl.BlockSpec((tm, w), row) for w in widths]
    out_shape = [jax.ShapeDtypeStruct((tokens, w), BF16) for w in widths]
    out_specs[5] = pl.BlockSpec((None, B_KV, tm), lambda i: (i // per_kv, 0, i % per_kv))
    out_shape[5] = jax.ShapeDtypeStruct((tokens // kv_block, B_KV, kv_block), BF16)
    return pl.pallas_call(
        _inproj_kernel,
        grid=(tokens // tm,),
        in_specs=[pl.BlockSpec((tm, d), row),
                  _resident((1, d), fixed),
                  _resident(w_bf.shape, fixed),
                  pl.BlockSpec((tm, HEAD_DIM), tab), pl.BlockSpec((tm, HEAD_DIM), tab),
                  pl.BlockSpec((tm, HEAD_DIM), tab), pl.BlockSpec((tm, HEAD_DIM), tab),
                  _resident((1, HEAD_DIM), fixed), _resident((1, HEAD_DIM), fixed)],
        out_specs=out_specs,
        out_shape=out_shape,
        compiler_params=_params("parallel"),
        name="inproj",
    )(x2, g.reshape(1, d), w_bf, ca, sa, cb, sb, qg.reshape(1, HEAD_DIM), kg.reshape(1, HEAD_DIM))


def _memkv_kernel(mem_ref, g_ref, w_ref, km_ref, vm_ref):
    h = _rms(mem_ref[...], g_ref[...]).astype(BF16)
    kv = _dot(h, w_ref[...])
    km_ref[...] = kv[:, :M_Q].astype(km_ref.dtype)
    vm_ref[...] = kv[:, M_Q:].astype(vm_ref.dtype)


def _memkv(mem, g, w_bf):
    b, n, d = mem.shape
    blk = lambda i: (i, 0, 0)
    fixed = lambda i: (0, 0)
    return pl.pallas_call(
        _memkv_kernel,
        grid=(b,),
        in_specs=[pl.BlockSpec((None, n, d), blk), _resident((1, d), fixed), _resident(w_bf.shape, fixed)],
        out_specs=[pl.BlockSpec((None, n, M_Q), blk), pl.BlockSpec((None, n, M_Q), blk)],
        out_shape=[jax.ShapeDtypeStruct((b, n, M_Q), BF16)] * 2,
        compiler_params=_params("parallel"),
        name="memkv",
    )(mem, g.reshape(1, d), w_bf)


def _local_kernel(sink_ref, qa_ref, kp_ref, kc_ref, kn_ref, vp_ref, vc_ref, vn_ref,
                  qm_ref, km_ref, vm_ref, ga_ref, gm_ref, oa_ref, om_ref,
                  kbuf, vbuf, vmbuf, sbuf, smbuf, obuf, mbuf):
    bq = qa_ref.shape[0]
    nsub = bq // WINDOW
    i = pl.program_id(1)
    first = i == 0
    last = i == pl.num_programs(1) - 1
    group = A_HEADS // A_KV_HEADS
    cols = 3 * WINDOW
    two = 2 * HEAD_DIM

    kbuf[0:WINDOW] = kp_ref[...]
    kbuf[WINDOW:WINDOW + bq] = kc_ref[...]
    kbuf[WINDOW + bq:] = kn_ref[...]
    for g in range(A_KV_HEADS):
        sl = slice(g * HEAD_DIM, (g + 1) * HEAD_DIM)
        vbuf[0:WINDOW, g * two:g * two + HEAD_DIM] = vp_ref[:, sl]
        vbuf[WINDOW:WINDOW + bq, g * two:g * two + HEAD_DIM] = vc_ref[:, sl]
        vbuf[WINDOW + bq:, g * two:g * two + HEAD_DIM] = vn_ref[:, sl]
        vbuf[:, g * two + HEAD_DIM:(g + 1) * two] = jnp.ones((bq + 2 * WINDOW, HEAD_DIM), vbuf.dtype)
    for h in range(M_HEADS):
        vmbuf[:, h * two:h * two + HEAD_DIM] = vm_ref[:, h * HEAD_DIM:(h + 1) * HEAD_DIM]
        vmbuf[:, h * two + HEAD_DIM:(h + 1) * two] = jnp.ones((vm_ref.shape[0], HEAD_DIM), vmbuf.dtype)

    for j in range(nsub):
        for g in range(A_KV_HEADS):
            q = jnp.concatenate(
                [qa_ref[j * WINDOW:(j + 1) * WINDOW, (g * group + h) * HEAD_DIM:(g * group + h + 1) * HEAD_DIM]
                 for h in range(group)], axis=0)
            sbuf[j * A_KV_HEADS + g] = _dot_nt(q, kbuf[j * WINDOW:j * WINDOW + cols, g * HEAD_DIM:(g + 1) * HEAD_DIM])
    for h in range(M_HEADS):
        sl = slice(h * HEAD_DIM, (h + 1) * HEAD_DIM)
        smbuf[h] = _dot_nt(qm_ref[:, sl], km_ref[:, sl])

    qi = lax.broadcasted_iota(jnp.int32, (WINDOW, WINDOW), 0)
    kc = lax.broadcasted_iota(jnp.int32, (WINDOW, WINDOW), 1)
    neg = jnp.float32(NEG_INF)
    tile_heads = lambda t: jnp.concatenate([t] * group, axis=0)
    bias_left = jnp.where(kc >= qi, 0.0, neg)
    bias_right = jnp.where(kc <= qi, 0.0, neg)
    bias_left_first = tile_heads(jnp.where(first, neg, bias_left))
    bias_right_last = tile_heads(jnp.where(last, neg, bias_right))
    bias_left, bias_right = tile_heads(bias_left), tile_heads(bias_right)

    def softmax_pv(chunks, m, v):
        p = jnp.concatenate([jnp.exp2(c - m) for c in chunks], axis=1).astype(BF16)
        pv = _dot(p, v)
        return pv[:, :HEAD_DIM], pv[:, HEAD_DIM:]

    for j in range(nsub):
        for g in range(A_KV_HEADS):
            s = sbuf[j * A_KV_HEADS + g]
            chunks = [s[:, 0:WINDOW] + (bias_left_first if j == 0 else bias_left),
                      s[:, WINDOW:2 * WINDOW],
                      s[:, 2 * WINDOW:] + (bias_right_last if j == nsub - 1 else bias_right)]
            sink = jnp.concatenate(
                [jnp.full((WINDOW, HEAD_DIM), sink_ref[g * group + h] * LOG2E, F32) for h in range(group)], axis=0)
            m = jnp.maximum(jnp.max(functools.reduce(jnp.maximum, chunks), axis=-1, keepdims=True), sink)
            num, den = softmax_pv(chunks, m, vbuf[j * WINDOW:j * WINDOW + cols, g * two:(g + 1) * two])
            o = num * (1.0 / (den + jnp.exp2(sink - m)))
            for h in range(group):
                c0 = (g * group + h) * HEAD_DIM
                obuf[j * WINDOW:(j + 1) * WINDOW, c0:c0 + HEAD_DIM] = o[h * WINDOW:(h + 1) * WINDOW]
    oa_ref[...] = _rms(obuf[...], ga_ref[...]).astype(oa_ref.dtype)

    for h in range(M_HEADS):
        s = smbuf[h]
        chunks = [s[:, c * HEAD_DIM:(c + 1) * HEAD_DIM] for c in range(s.shape[1] // HEAD_DIM)]
        m = jnp.max(functools.reduce(jnp.maximum, chunks), axis=-1, keepdims=True)
        m = jnp.broadcast_to(m, chunks[0].shape)
        num, den = softmax_pv(chunks, m, vmbuf[:, h * two:(h + 1) * two])
        mbuf[:, h * HEAD_DIM:(h + 1) * HEAD_DIM] = num * (1.0 / den)
    om_ref[...] = _rms(mbuf[...], gm_ref[...]).astype(om_ref.dtype)


def _local_attention(qa, ka, va, qm, km, vm, sink, ga, gm, bq):
    b, seq, _ = qa.shape
    nt = seq // bq
    sub = bq // WINDOW
    nblk = seq // WINDOW
    n_mem = km.shape[1]
    tile = lambda bi, i: (bi, i, 0)
    prev = lambda bi, i: (bi, jnp.maximum(i * sub - 1, 0), 0)
    nxt = lambda bi, i: (bi, jnp.minimum((i + 1) * sub, nblk - 1), 0)
    per_b = lambda bi, i: (bi, 0, 0)
    fixed = lambda bi, i: (0, 0)
    kv_specs = [pl.BlockSpec((None, WINDOW, A_KV), prev), pl.BlockSpec((None, bq, A_KV), tile),
                pl.BlockSpec((None, WINDOW, A_KV), nxt)]
    return pl.pallas_call(
        _local_kernel,
        grid=(b, nt),
        in_specs=[pl.BlockSpec(memory_space=pltpu.SMEM),
                  pl.BlockSpec((None, bq, A_Q), tile)] + kv_specs + kv_specs + [
                  pl.BlockSpec((None, bq, M_Q), tile),
                  pl.BlockSpec((None, n_mem, M_Q), per_b), pl.BlockSpec((None, n_mem, M_Q), per_b),
                  _resident((1, A_Q), fixed), _resident((1, M_Q), fixed)],
        out_specs=[pl.BlockSpec((None, bq, A_Q), tile), pl.BlockSpec((None, bq, M_Q), tile)],
        out_shape=[jax.ShapeDtypeStruct((b, seq, A_Q), BF16), jax.ShapeDtypeStruct((b, seq, M_Q), BF16)],
        scratch_shapes=[pltpu.VMEM((bq + 2 * WINDOW, A_KV), BF16),
                        pltpu.VMEM((bq + 2 * WINDOW, 2 * A_KV), BF16),
                        pltpu.VMEM((n_mem, 2 * M_Q), BF16),
                        pltpu.VMEM((sub * A_KV_HEADS, (A_HEADS // A_KV_HEADS) * WINDOW, 3 * WINDOW), F32),
                        pltpu.VMEM((M_HEADS, bq, n_mem), F32),
                        pltpu.VMEM((bq, A_Q), F32), pltpu.VMEM((bq, M_Q), F32)],
        compiler_params=_params("parallel", "parallel"),
        name="local_attn",
    )(sink, qa, ka, ka, ka, va, va, va, qm, km, vm, ga.reshape(1, A_Q), gm.reshape(1, M_Q))


def _global_heads(q_ref, k_ref, vt_ref, obuf, bounded):
    bq = q_ref.shape[0]
    n_kv, _, bk = vt_ref.shape
    group = B_HEADS // B_KV_HEADS
    rows = group * bq
    sub = F32_SUBLANES
    l0 = jnp.zeros((sub, rows), F32)
    acc0 = jnp.zeros((HEAD_DIM, rows), F32)
    kv_heads = range(B_KV_HEADS)
    ksl = [slice(g * HEAD_DIM, (g + 1) * HEAD_DIM) for g in kv_heads]
    qs = [jnp.concatenate([q_ref[:, (g * group + h) * HEAD_DIM:(g * group + h + 1) * HEAD_DIM]
                           for h in range(group)], axis=0) for g in kv_heads]

    def logits_t(g, j):
        start = pl.multiple_of(j * bk, bk)
        return _dot_nt(k_ref[pl.ds(start, bk), ksl[g]], qs[g]).reshape(bk // sub, sub, rows)

    def pv_t(g, j, pt):
        return _dot(vt_ref[j, ksl[g], :], pt.reshape(bk, rows).astype(BF16))

    def bounded_body(j, carry):
        out = []
        for g, (l, acc) in zip(kv_heads, carry):
            pt = jnp.exp2(logits_t(g, j))
            out.append((l + jnp.sum(pt, axis=0), acc + pv_t(g, j, pt)))
        return tuple(out)

    def online_body(j, carry):
        out = []
        for g, (m, l, acc) in zip(kv_heads, carry):
            st = logits_t(g, j)
            m_new = jnp.maximum(m, jnp.max(jnp.max(st, axis=0), axis=0, keepdims=True))
            alpha = jnp.exp2(m - m_new)
            pt = jnp.exp2(st - m_new[None])
            out.append((m_new, alpha * l + jnp.sum(pt, axis=0), alpha[:1] * acc + pv_t(g, j, pt)))
        return tuple(out)

    if bounded:
        res = lax.fori_loop(0, n_kv, bounded_body, ((l0, acc0),) * B_KV_HEADS, unroll=min(n_kv, KV_UNROLL))
    else:
        m0 = jnp.full((sub, rows), NEG_INF, F32)
        res = [r[1:] for r in lax.fori_loop(0, n_kv, online_body, ((m0, l0, acc0),) * B_KV_HEADS)]
    for g, (l, acc) in zip(kv_heads, res):
        o = (acc * (1.0 / jnp.sum(l, axis=0, keepdims=True))).T
        for h in range(group):
            c0 = (g * group + h) * HEAD_DIM
            obuf[:, c0:c0 + HEAD_DIM] = o[h * bq:(h + 1) * bq]


def _global_kernel(q_ref, k_ref, vt_ref, g_ref, o_ref, obuf, *, bounded):
    _global_heads(q_ref, k_ref, vt_ref, obuf, bounded)
    o_ref[...] = _rms(obuf[...], g_ref[...]).astype(o_ref.dtype)


def _logits_bounded(qg, kg):
    bound = (HEAD_DIM ** 0.5) * jnp.max(jnp.abs(qg)) * jnp.max(jnp.abs(kg)) * ROUNDING_SLACK
    return bound <= SOFTMAX_BOUND_LIMIT


def _global_attention(q, k, vt, gb, bounded, bq):
    b, seq, _ = q.shape
    n_kv, _, bk = vt.shape[1:]
    tile = lambda bi, i: (bi, i, 0)
    fixed = lambda bi, i: (0, 0)

    def call(is_bounded):
        return pl.pallas_call(
            functools.partial(_global_kernel, bounded=is_bounded),
            grid=(b, seq // bq),
            in_specs=[pl.BlockSpec((None, bq, B_Q), tile),
                      _resident((None, seq, B_KV), lambda bi, i: (bi, 0, 0)),
                      _resident((None, n_kv, B_KV, bk), lambda bi, i: (bi, 0, 0, 0)),
                      _resident((1, B_Q), fixed)],
            out_specs=pl.BlockSpec((None, bq, B_Q), tile),
            out_shape=jax.ShapeDtypeStruct((b, seq, B_Q), BF16),
            scratch_shapes=[pltpu.VMEM((bq, B_Q), F32)],
            compiler_params=_params("parallel", "parallel"),
            name="global_attn_bounded" if is_bounded else "global_attn_online",
        )

    return lax.cond(bounded, call(True), call(False), q, k, vt, gb.reshape(1, B_Q))


def _outproj_kernel(oa_ref, ob_ref, om_ref, x_ref, w_ref, x1_ref):
    for r in range(x_ref.shape[0] // ROW_BLOCK):
        b = slice(r * ROW_BLOCK, (r + 1) * ROW_BLOCK)
        merged = jnp.concatenate([oa_ref[b, :], ob_ref[b, :], om_ref[b, :]], axis=1)
        x1_ref[b, :] = x_ref[b, :] + _dot(merged, w_ref[...])


def _outproj(oa, ob, om, x2, w_bf, tm):
    tokens, d = x2.shape
    row = lambda i: (i, 0)
    fixed = lambda i: (0, 0)
    return pl.pallas_call(
        _outproj_kernel,
        grid=(tokens // tm,),
        in_specs=[pl.BlockSpec((tm, A_Q), row), pl.BlockSpec((tm, B_Q), row), pl.BlockSpec((tm, M_Q), row),
                  pl.BlockSpec((tm, d), row), _resident(w_bf.shape, fixed)],
        out_specs=pl.BlockSpec((tm, d), row),
        out_shape=jax.ShapeDtypeStruct((tokens, d), F32),
        compiler_params=_params("parallel"),
        name="outproj",
    )(oa, ob, om, x2, w_bf)


def _ffn_kernel(x1_ref, gf_ref, wg_ref, wu_ref, wd_ref, gl_ref, o_ref, h_scr):
    j = pl.program_id(1)
    last = pl.num_programs(1) - 1
    n_blocks = x1_ref.shape[0] // ROW_BLOCK
    rows = lambda r: slice(r * ROW_BLOCK, (r + 1) * ROW_BLOCK)

    def chunk(first, final):
        def gate_up(r):
            if first:
                h_scr[rows(r)] = _rms(x1_ref[rows(r)], gf_ref[...]).astype(h_scr.dtype)
            h = h_scr[rows(r)]
            return _dot(h, wg_ref[...]), _dot(h, wu_ref[...])

        pending = gate_up(0)
        for r in range(n_blocks):
            ahead = gate_up(r + 1) if r + 1 < n_blocks else None
            gate, up = pending
            act = (gate * jax.nn.sigmoid(gate) * up).astype(BF16)
            acc = (x1_ref[rows(r)] if first else o_ref[rows(r)]) + _dot(act, wd_ref[...])
            o_ref[rows(r)] = _rms(acc, gl_ref[...]) if final else acc
            pending = ahead

    pl.when(j == 0)(lambda: chunk(True, False))
    pl.when(jnp.logical_and(j > 0, j < last))(lambda: chunk(False, False))
    pl.when(j == last)(lambda: chunk(False, True))


def _ffn(x1, gf, w_gu_bf, w_dn_bf, gl, tm, tf):
    tokens, d = x1.shape
    d_ff = w_dn_bf.shape[0]
    nf = d_ff // tf
    assert nf >= 2 and tm % ROW_BLOCK == 0
    row = lambda i, j: (i, 0)
    fixed = lambda i, j: (0, 0)
    return pl.pallas_call(
        _ffn_kernel,
        grid=(tokens // tm, nf),
        in_specs=[pl.BlockSpec((tm, d), row), _resident((1, d), fixed),
                  pl.BlockSpec((d, tf), lambda i, j: (0, j)), pl.BlockSpec((d, tf), lambda i, j: (0, j + nf)),
                  pl.BlockSpec((tf, d), lambda i, j: (j, 0)), _resident((1, d), fixed)],
        out_specs=pl.BlockSpec((tm, d), row),
        out_shape=jax.ShapeDtypeStruct((tokens, d), F32),
        scratch_shapes=[pltpu.VMEM((tm, d), BF16)],
        compiler_params=_params("parallel", "arbitrary"),
        name="ffn",
    )(x1, gf.reshape(1, d), w_gu_bf, w_gu_bf, w_dn_bf, gl.reshape(1, d))


def _tile(n, pref):
    t = min(pref, n)
    assert n % t == 0, (n, t)
    return t


def _tiles(seq, d_ff):
    return dict(
        proj=_tile(seq, 512),
        local=_tile(seq, 512),
        global_q=_tile(seq, 512),
        global_kv=_tile(seq, 4096),
        ffn=_tile(seq, 1024),
        ffn_cols=_tile(d_ff, 512),
    )


def _layer(x, mem, rope, norm_mix_g, norm_mem_g, w_in, w_mem_kv, sink_a, q_norm_b_g, k_norm_b_g,
           out_norm_g, w_out, norm_ffn_g, w_gate_up, w_down, final_g):
    b, seq, d = x.shape
    tokens = b * seq
    assert seq % GRID_W == 0 and seq % WINDOW == 0
    x2 = x.reshape(tokens, d)
    t = _tiles(seq, w_down.shape[0])

    qa, ka, va, qb, kb, vbt, qm = _inproj(x2, seq, norm_mix_g, w_in.astype(BF16), q_norm_b_g, k_norm_b_g,
                                          rope, t["proj"], t["global_kv"])
    km, vm = _memkv(mem, norm_mem_g, w_mem_kv.astype(BF16))

    r3 = lambda a: a.reshape(b, seq, a.shape[-1])
    oa, om = _local_attention(r3(qa), r3(ka), r3(va), r3(qm), km, vm, sink_a,
                              out_norm_g[:A_Q], out_norm_g[A_Q + B_Q:], t["local"])
    vbt = vbt.reshape(b, seq // t["global_kv"], B_KV, t["global_kv"])
    ob = _global_attention(r3(qb), r3(kb), vbt, out_norm_g[A_Q:A_Q + B_Q],
                           _logits_bounded(q_norm_b_g, k_norm_b_g), t["global_q"])

    x1 = _outproj(oa.reshape(tokens, A_Q), ob.reshape(tokens, B_Q), om.reshape(tokens, M_Q), x2,
                  w_out.astype(BF16), t["proj"])
    y = _ffn(x1, norm_ffn_g, w_gate_up.astype(BF16), w_down.astype(BF16), final_g, t["ffn"], t["ffn_cols"])
    return y.reshape(b, seq, d)


def kernel(x_prompt, x_sample, mem_prompt, mem_sample, norm_mix_g, norm_mem_g, w_in, w_mem_kv, sink_a,
           q_norm_b_g, k_norm_b_g, out_norm_g, w_out, norm_ffn_g, w_gate_up, w_down, norm_final_g):
    depth = w_in.shape[0]
    assert depth == 1, "final norm is fused into the last layer's ffn kernel"
    args = (norm_mix_g[0], norm_mem_g[0], w_in[0], w_mem_kv[0], sink_a[0], q_norm_b_g[0], k_norm_b_g[0],
            out_norm_g[0], w_out[0], norm_ffn_g[0], w_gate_up[0], w_down[0], norm_final_g)
    rope = _rope_tables(max(x_prompt.shape[1], x_sample.shape[1]))
    return (_layer(x_prompt, mem_prompt, rope, *args), _layer(x_sample, mem_sample, rope, *args))
```

```python
import functools

import jax
import jax.numpy as jnp
from jax import lax
from jax.experimental import pallas as pl
from jax.experimental.pallas import tpu as pltpu

HEAD_DIM = 128
A_HEADS, A_KV_HEADS = 8, 2
B_HEADS, B_KV_HEADS = 4, 2
M_HEADS = 4
WINDOW = 128
GRID_W = 64
ROPE_THETA = 10000.0
NORM_EPS = 1e-6
NEG_INF = -1e30
A_Q, A_KV = A_HEADS * HEAD_DIM, A_KV_HEADS * HEAD_DIM
B_Q, B_KV = B_HEADS * HEAD_DIM, B_KV_HEADS * HEAD_DIM
M_Q = M_HEADS * HEAD_DIM
LOG2E = 1.4426950408889634
Q_SCALE = HEAD_DIM ** -0.5 * LOG2E
SOFTMAX_BOUND_LIMIT = 32.0
ROUNDING_SLACK = 1.02
KV_UNROLL = 2
ROW_BLOCK = 256

BF16 = jnp.bfloat16
F32 = jnp.float32
F32_SUBLANES = 8

VMEM_LIMIT_BYTES = 56 * 1024 * 1024


def _params(*sem):
    return pltpu.CompilerParams(dimension_semantics=sem, vmem_limit_bytes=VMEM_LIMIT_BYTES)


def _resident(shape, index_map):
    return pl.BlockSpec(shape, index_map, pipeline_mode=pl.Buffered(1))


def _rms(x, g):
    ms = jnp.mean(x * x, axis=-1, keepdims=True)
    return x * lax.rsqrt(ms + NORM_EPS) * g


def _dot(a, b):
    return jnp.dot(a, b, preferred_element_type=F32)


def _dot_nt(a, b):
    return lax.dot_general(a, b, (((1,), (1,)), ((), ())), preferred_element_type=F32)


def _inproj_kernel(x_ref, g_ref, w_ref, ca_ref, sa_ref, cb_ref, sb_ref, qg_ref, kg_ref,
                   qa_ref, ka_ref, va_ref, qb_ref, kb_ref, vbt_ref, qm_ref):
    h = _rms(x_ref[...], g_ref[...]).astype(BF16)
    ca, sa, cb, sb = ca_ref[...], sa_ref[...], cb_ref[...], sb_ref[...]
    lane = lax.broadcasted_iota(jnp.int32, ca.shape, 1)
    low_quarter = (lane % (HEAD_DIM // 2)) < (HEAD_DIM // 4)

    def rope_a(t):
        return t * ca + pltpu.roll(t, HEAD_DIM // 2, 1) * sa

    def rope_b(t):
        partner = jnp.where(low_quarter, pltpu.roll(t, HEAD_DIM - HEAD_DIM // 4, 1),
                            pltpu.roll(t, HEAD_DIM // 4, 1))
        return t * cb + partner * sb

    def heads(t, n, fn, out_ref):
        for i in range(n):
            sl = slice(i * HEAD_DIM, (i + 1) * HEAD_DIM)
            out_ref[:, sl] = fn(t[:, sl]).astype(out_ref.dtype)

    widths = (A_Q, A_KV, A_KV, B_Q, B_KV, B_KV, M_Q)
    starts = [sum(widths[:i]) for i in range(len(widths))]

    def section(i):
        return _dot(h, w_ref[:, starts[i]:starts[i] + widths[i]])

    qg, kg = qg_ref[...], kg_ref[...]
    heads(section(3), B_HEADS, lambda t: rope_b(_rms(t, qg)) * Q_SCALE, qb_ref)
    heads(section(4), B_KV_HEADS, lambda t: rope_b(_rms(t, kg)), kb_ref)
    vbt_ref[...] = section(5).astype(vbt_ref.dtype).T
    heads(section(0), A_HEADS, lambda t: rope_a(t) * Q_SCALE, qa_ref)
    heads(section(1), A_KV_HEADS, rope_a, ka_ref)
    qm_ref[...] = (section(6) * Q_SCALE).astype(qm_ref.dtype)
    va_ref[...] = section(2).astype(va_ref.dtype)


def _rope_tables(seq):
    def tables(n, dim):
        inv = ROPE_THETA ** (-(jnp.arange(0, dim, 2, dtype=F32) / dim))
        ang = jnp.arange(n, dtype=F32)[:, None] * jnp.tile(inv, 2)[None, :]
        sign = jnp.where(jnp.arange(dim) < dim // 2, -1.0, 1.0).astype(F32)
        return jnp.cos(ang), jnp.sin(ang) * sign

    ca, sa = tables(seq, HEAD_DIM)
    n_rows = seq // GRID_W
    half = HEAD_DIM // 2
    per_row = lambda t: jnp.broadcast_to(t[:, None, :], (n_rows, GRID_W, half))
    per_col = lambda t: jnp.broadcast_to(t[None, :, :], (n_rows, GRID_W, half))
    cr, sr = tables(n_rows, half)
    cc, sc = tables(GRID_W, half)
    cb = jnp.concatenate([per_row(cr), per_col(cc)], axis=-1).reshape(seq, HEAD_DIM)
    sb = jnp.concatenate([per_row(sr), per_col(sc)], axis=-1).reshape(seq, HEAD_DIM)
    return ca, sa, cb, sb


def _inproj(x2, seq, g, w_bf, qg, kg, rope, tm, kv_block):
    tokens, d = x2.shape
    per_seq = seq // tm
    per_kv = kv_block // tm
    assert kv_block % tm == 0 and seq % kv_block == 0
    ca, sa, cb, sb = rope
    row = lambda i: (i, 0)
    fixed = lambda i: (0, 0)
    tab = lambda i: (i % per_seq, 0)
    widths = (A_Q, A_KV, A_KV, B_Q, B_KV, B_KV, M_Q)
    out_specs = [pl.BlockSpec((tm, w), row) for w in widths]
    out_shape = [jax.ShapeDtypeStruct((tokens, w), BF16) for w in widths]
    out_specs[5] = pl.BlockSpec((None, B_KV, tm), lambda i: (i // per_kv, 0, i % per_kv))
    out_shape[5] = jax.ShapeDtypeStruct((tokens // kv_block, B_KV, kv_block), BF16)
    return pl.pallas_call(
        _inproj_kernel,
        grid=(tokens // tm,),
        in_specs=[pl.BlockSpec((tm, d), row),
                  _resident((1, d), fixed),
                  _resident(w_bf.shape, fixed),
                  pl.BlockSpec((tm, HEAD_DIM), tab), pl.BlockSpec((tm, HEAD_DIM), tab),
                  pl.BlockSpec((tm, HEAD_DIM), tab), pl.BlockSpec((tm, HEAD_DIM), tab),
                  _resident((1, HEAD_DIM), fixed), _resident((1, HEAD_DIM), fixed)],
        out_specs=out_specs,
        out_shape=out_shape,
        compiler_params=_params("parallel"),
        name="inproj",
    )(x2, g.reshape(1, d), w_bf, ca, sa, cb, sb, qg.reshape(1, HEAD_DIM), kg.reshape(1, HEAD_DIM))


def _memkv_kernel(mem_ref, g_ref, w_ref, km_ref, vm_ref):
    h = _rms(mem_ref[...], g_ref[...]).astype(BF16)
    kv = _dot(h, w_ref[...])
    km_ref[...] = kv[:, :M_Q].astype(km_ref.dtype)
    vm_ref[...] = kv[:, M_Q:].astype(vm_ref.dtype)


def _memkv(mem, g, w_bf):
    b, n, d = mem.shape
    blk = lambda i: (i, 0, 0)
    fixed = lambda i: (0, 0)
    return pl.pallas_call(
        _memkv_kernel,
        grid=(b,),
        in_specs=[pl.BlockSpec((None, n, d), blk), _resident((1, d), fixed), _resident(w_bf.shape, fixed)],
        out_specs=[pl.BlockSpec((None, n, M_Q), blk), pl.BlockSpec((None, n, M_Q), blk)],
        out_shape=[jax.ShapeDtypeStruct((b, n, M_Q), BF16)] * 2,
        compiler_params=_params("parallel"),
        name="memkv",
    )(mem, g.reshape(1, d), w_bf)


def _local_kernel(sink_ref, qa_ref, kp_ref, kc_ref, kn_ref, vp_ref, vc_ref, vn_ref,
                  qm_ref, km_ref, vm_ref, ga_ref, gm_ref, oa_ref, om_ref,
                  kbuf, vbuf, vmbuf, sbuf, smbuf, obuf, mbuf):
    bq = qa_ref.shape[0]
    nsub = bq // WINDOW
    i = pl.program_id(1)
    first = i == 0
    last = i == pl.num_programs(1) - 1
    group = A_HEADS // A_KV_HEADS
    cols = 3 * WINDOW
    two = 2 * HEAD_DIM

    kbuf[0:WINDOW] = kp_ref[...]
    kbuf[WINDOW:WINDOW + bq] = kc_ref[...]
    kbuf[WINDOW + bq:] = kn_ref[...]
    for g in range(A_KV_HEADS):
        sl = slice(g * HEAD_DIM, (g + 1) * HEAD_DIM)
        vbuf[0:WINDOW, g * two:g * two + HEAD_DIM] = vp_ref[:, sl]
        vbuf[WINDOW:WINDOW + bq, g * two:g * two + HEAD_DIM] = vc_ref[:, sl]
        vbuf[WINDOW + bq:, g * two:g * two + HEAD_DIM] = vn_ref[:, sl]
        vbuf[:, g * two + HEAD_DIM:(g + 1) * two] = jnp.ones((bq + 2 * WINDOW, HEAD_DIM), vbuf.dtype)
    for h in range(M_HEADS):
        vmbuf[:, h * two:h * two + HEAD_DIM] = vm_ref[:, h * HEAD_DIM:(h + 1) * HEAD_DIM]
        vmbuf[:, h * two + HEAD_DIM:(h + 1) * two] = jnp.ones((vm_ref.shape[0], HEAD_DIM), vmbuf.dtype)

    for j in range(nsub):
        for g in range(A_KV_HEADS):
            q = jnp.concatenate(
                [qa_ref[j * WINDOW:(j + 1) * WINDOW, (g * group + h) * HEAD_DIM:(g * group + h + 1) * HEAD_DIM]
                 for h in range(group)], axis=0)
            sbuf[j * A_KV_HEADS + g] = _dot_nt(q, kbuf[j * WINDOW:j * WINDOW + cols, g * HEAD_DIM:(g + 1) * HEAD_DIM])
    for h in range(M_HEADS):
        sl = slice(h * HEAD_DIM, (h + 1) * HEAD_DIM)
        smbuf[h] = _dot_nt(qm_ref[:, sl], km_ref[:, sl])

    qi = lax.broadcasted_iota(jnp.int32, (WINDOW, WINDOW), 0)
    kc = lax.broadcasted_iota(jnp.int32, (WINDOW, WINDOW), 1)
    neg = jnp.float32(NEG_INF)
    tile_heads = lambda t: jnp.concatenate([t] * group, axis=0)
    bias_left = jnp.where(kc >= qi, 0.0, neg)
    bias_right = jnp.where(kc <= qi, 0.0, neg)
    bias_left_first = tile_heads(jnp.where(first, neg, bias_left))
    bias_right_last = tile_heads(jnp.where(last, neg, bias_right))
    bias_left, bias_right = tile_heads(bias_left), tile_heads(bias_right)

    def softmax_pv(chunks, m, v):
        p = jnp.concatenate([jnp.exp2(c - m) for c in chunks], axis=1).astype(BF16)
        pv = _dot(p, v)
        return pv[:, :HEAD_DIM], pv[:, HEAD_DIM:]

    for j in range(nsub):
        for g in range(A_KV_HEADS):
            s = sbuf[j * A_KV_HEADS + g]
            chunks = [s[:, 0:WINDOW] + (bias_left_first if j == 0 else bias_left),
                      s[:, WINDOW:2 * WINDOW],
                      s[:, 2 * WINDOW:] + (bias_right_last if j == nsub - 1 else bias_right)]
            sink = jnp.concatenate(
                [jnp.full((WINDOW, HEAD_DIM), sink_ref[g * group + h] * LOG2E, F32) for h in range(group)], axis=0)
            m = jnp.maximum(jnp.max(functools.reduce(jnp.maximum, chunks), axis=-1, keepdims=True), sink)
            num, den = softmax_pv(chunks, m, vbuf[j * WINDOW:j * WINDOW + cols, g * two:(g + 1) * two])
            o = num * (1.0 / (den + jnp.exp2(sink - m)))
            for h in range(group):
                c0 = (g * group + h) * HEAD_DIM
                obuf[j * WINDOW:(j + 1) * WINDOW, c0:c0 + HEAD_DIM] = o[h * WINDOW:(h + 1) * WINDOW]
    oa_ref[...] = _rms(obuf[...], ga_ref[...]).astype(oa_ref.dtype)

    for h in range(M_HEADS):
        s = smbuf[h]
        chunks = [s[:, c * HEAD_DIM:(c + 1) * HEAD_DIM] for c in range(s.shape[1] // HEAD_DIM)]
        m = jnp.max(functools.reduce(jnp.maximum, chunks), axis=-1, keepdims=True)
        m = jnp.broadcast_to(m, chunks[0].shape)
        num, den = softmax_pv(chunks, m, vmbuf[:, h * two:(h + 1) * two])
        mbuf[:, h * HEAD_DIM:(h + 1) * HEAD_DIM] = num * (1.0 / den)
    om_ref[...] = _rms(mbuf[...], gm_ref[...]).astype(om_ref.dtype)


def _local_attention(qa, ka, va, qm, km, vm, sink, ga, gm, bq):
    b, seq, _ = qa.shape
    nt = seq // bq
    sub = bq // WINDOW
    nblk = seq // WINDOW
    n_mem = km.shape[1]
    tile = lambda bi, i: (bi, i, 0)
    prev = lambda bi, i: (bi, jnp.maximum(i * sub - 1, 0), 0)
    nxt = lambda bi, i: (bi, jnp.minimum((i + 1) * sub, nblk - 1), 0)
    per_b = lambda bi, i: (bi, 0, 0)
    fixed = lambda bi, i: (0, 0)
    kv_specs = [pl.BlockSpec((None, WINDOW, A_KV), prev), pl.BlockSpec((None, bq, A_KV), tile),
                pl.BlockSpec((None, WINDOW, A_KV), nxt)]
    return pl.pallas_call(
        _local_kernel,
        grid=(b, nt),
        in_specs=[pl.BlockSpec(memory_space=pltpu.SMEM),
                  pl.BlockSpec((None, bq, A_Q), tile)] + kv_specs + kv_specs + [
                  pl.BlockSpec((None, bq, M_Q), tile),
                  pl.BlockSpec((None, n_mem, M_Q), per_b), pl.BlockSpec((None, n_mem, M_Q), per_b),
                  _resident((1, A_Q), fixed), _resident((1, M_Q), fixed)],
        out_specs=[pl.BlockSpec((None, bq, A_Q), tile), pl.BlockSpec((None, bq, M_Q), tile)],
        out_shape=[jax.ShapeDtypeStruct((b, seq, A_Q), BF16), jax.ShapeDtypeStruct((b, seq, M_Q), BF16)],
        scratch_shapes=[pltpu.VMEM((bq + 2 * WINDOW, A_KV), BF16),
                        pltpu.VMEM((bq + 2 * WINDOW, 2 * A_KV), BF16),
                        pltpu.VMEM((n_mem, 2 * M_Q), BF16),
                        pltpu.VMEM((sub * A_KV_HEADS, (A_HEADS // A_KV_HEADS) * WINDOW, 3 * WINDOW), F32),
                        pltpu.VMEM((M_HEADS, bq, n_mem), F32),
                        pltpu.VMEM((bq, A_Q), F32), pltpu.VMEM((bq, M_Q), F32)],
        compiler_params=_params("parallel", "parallel"),
        name="local_attn",
    )(sink, qa, ka, ka, ka, va, va, va, qm, km, vm, ga.reshape(1, A_Q), gm.reshape(1, M_Q))


def _global_heads(q_ref, k_ref, vt_ref, obuf, bounded):
    bq = q_ref.shape[0]
    n_kv, _, bk = vt_ref.shape
    group = B_HEADS // B_KV_HEADS
    rows = group * bq
    sub = F32_SUBLANES
    l0 = jnp.zeros((sub, rows), F32)
    acc0 = jnp.zeros((HEAD_DIM, rows), F32)
    kv_heads = range(B_KV_HEADS)
    ksl = [slice(g * HEAD_DIM, (g + 1) * HEAD_DIM) for g in kv_heads]
    qs = [jnp.concatenate([q_ref[:, (g * group + h) * HEAD_DIM:(g * group + h + 1) * HEAD_DIM]
                           for h in range(group)], axis=0) for g in kv_heads]

    def logits_t(g, j):
        start = pl.multiple_of(j * bk, bk)
        return _dot_nt(k_ref[pl.ds(start, bk), ksl[g]], qs[g]).reshape(bk // sub, sub, rows)

    def pv_t(g, j, pt):
        return _dot(vt_ref[j, ksl[g], :], pt.reshape(bk, rows).astype(BF16))

    def bounded_body(j, carry):
        out = []
        for g, (l, acc) in zip(kv_heads, carry):
            pt = jnp.exp2(logits_t(g, j))
            out.append((l + jnp.sum(pt, axis=0), acc + pv_t(g, j, pt)))
        return tuple(out)

    def online_body(j, carry):
        out = []
        for g, (m, l, acc) in zip(kv_heads, carry):
            st = logits_t(g, j)
            m_new = jnp.maximum(m, jnp.max(jnp.max(st, axis=0), axis=0, keepdims=True))
            alpha = jnp.exp2(m - m_new)
            pt = jnp.exp2(st - m_new[None])
            out.append((m_new, alpha * l + jnp.sum(pt, axis=0), alpha[:1] * acc + pv_t(g, j, pt)))
        return tuple(out)

    if bounded:
        res = lax.fori_loop(0, n_kv, bounded_body, ((l0, acc0),) * B_KV_HEADS, unroll=min(n_kv, KV_UNROLL))
    else:
        m0 = jnp.full((sub, rows), NEG_INF, F32)
        res = [r[1:] for r in lax.fori_loop(0, n_kv, online_body, ((m0, l0, acc0),) * B_KV_HEADS)]
    for g, (l, acc) in zip(kv_heads, res):
        o = (acc * (1.0 / jnp.sum(l, axis=0, keepdims=True))).T
        for h in range(group):
            c0 = (g * group + h) * HEAD_DIM
            obuf[:, c0:c0 + HEAD_DIM] = o[h * bq:(h + 1) * bq]


def _global_kernel(q_ref, k_ref, vt_ref, g_ref, o_ref, obuf, *, bounded):
    _global_heads(q_ref, k_ref, vt_ref, obuf, bounded)
    o_ref[...] = _rms(obuf[...], g_ref[...]).astype(o_ref.dtype)


def _logits_bounded(qg, kg):
    bound = (HEAD_DIM ** 0.5) * jnp.max(jnp.abs(qg)) * jnp.max(jnp.abs(kg)) * ROUNDING_SLACK
    return bound <= SOFTMAX_BOUND_LIMIT


def _global_attention(q, k, vt, gb, bounded, bq):
    b, seq, _ = q.shape
    n_kv, _, bk = vt.shape[1:]
    tile = lambda bi, i: (bi, i, 0)
    fixed = lambda bi, i: (0, 0)

    def call(is_bounded):
        return pl.pallas_call(
            functools.partial(_global_kernel, bounded=is_bounded),
            grid=(b, seq // bq),
            in_specs=[pl.BlockSpec((None, bq, B_Q), tile),
                      _resident((None, seq, B_KV), lambda bi, i: (bi, 0, 0)),
                      _resident((None, n_kv, B_KV, bk), lambda bi, i: (bi, 0, 0, 0)),
                      _resident((1, B_Q), fixed)],
            out_specs=pl.BlockSpec((None, bq, B_Q), tile),
            out_shape=jax.ShapeDtypeStruct((b, seq, B_Q), BF16),
            scratch_shapes=[pltpu.VMEM((bq, B_Q), F32)],
            compiler_params=_params("parallel", "parallel"),
            name="global_attn_bounded" if is_bounded else "global_attn_online",
        )

    return lax.cond(bounded, call(True), call(False), q, k, vt, gb.reshape(1, B_Q))


def _outproj_kernel(oa_ref, ob_ref, om_ref, x_ref, w_ref, x1_ref):
    merged = jnp.concatenate([oa_ref[...], ob_ref[...], om_ref[...]], axis=1)
    x1_ref[...] = x_ref[...] + _dot(merged, w_ref[...])


def _outproj(oa, ob, om, x2, w_bf, tm):
    tokens, d = x2.shape
    row = lambda i: (i, 0)
    fixed = lambda i: (0, 0)
    return pl.pallas_call(
        _outproj_kernel,
        grid=(tokens // tm,),
        in_specs=[pl.BlockSpec((tm, A_Q), row), pl.BlockSpec((tm, B_Q), row), pl.BlockSpec((tm, M_Q), row),
                  pl.BlockSpec((tm, d), row), _resident(w_bf.shape, fixed)],
        out_specs=pl.BlockSpec((tm, d), row),
        out_shape=jax.ShapeDtypeStruct((tokens, d), F32),
        compiler_params=_params("parallel"),
        name="outproj",
    )(oa, ob, om, x2, w_bf)


def _ffn_kernel(x1_ref, gf_ref, wg_ref, wu_ref, wd_ref, gl_ref, o_ref, h_scr):
    j = pl.program_id(1)
    last = pl.num_programs(1) - 1
    n_blocks = x1_ref.shape[0] // ROW_BLOCK
    rows = lambda r: slice(r * ROW_BLOCK, (r + 1) * ROW_BLOCK)

    def chunk(first, final):
        def gate_up(r):
            if first:
                h_scr[rows(r)] = _rms(x1_ref[rows(r)], gf_ref[...]).astype(h_scr.dtype)
            h = h_scr[rows(r)]
            return _dot(h, wg_ref[...]), _dot(h, wu_ref[...])

        pending = gate_up(0)
        for r in range(n_blocks):
            ahead = gate_up(r + 1) if r + 1 < n_blocks else None
            gate, up = pending
            act = (gate * jax.nn.sigmoid(gate) * up).astype(BF16)
            acc = (x1_ref[rows(r)] if first else o_ref[rows(r)]) + _dot(act, wd_ref[...])
            o_ref[rows(r)] = _rms(acc, gl_ref[...]) if final else acc
            pending = ahead

    pl.when(j == 0)(lambda: chunk(True, False))
    pl.when(jnp.logical_and(j > 0, j < last))(lambda: chunk(False, False))
    pl.when(j == last)(lambda: chunk(False, True))


def _ffn(x1, gf, w_gu_bf, w_dn_bf, gl, tm, tf):
    tokens, d = x1.shape
    d_ff = w_dn_bf.shape[0]
    nf = d_ff // tf
    assert nf >= 2 and tm % ROW_BLOCK == 0
    row = lambda i, j: (i, 0)
    fixed = lambda i, j: (0, 0)
    return pl.pallas_call(
        _ffn_kernel,
        grid=(tokens // tm, nf),
        in_specs=[pl.BlockSpec((tm, d), row), _resident((1, d), fixed),
                  pl.BlockSpec((d, tf), lambda i, j: (0, j)), pl.BlockSpec((d, tf), lambda i, j: (0, j + nf)),
                  pl.BlockSpec((tf, d), lambda i, j: (j, 0)), _resident((1, d), fixed)],
        out_specs=pl.BlockSpec((tm, d), row),
        out_shape=jax.ShapeDtypeStruct((tokens, d), F32),
        scratch_shapes=[pltpu.VMEM((tm, d), BF16)],
        compiler_params=_params("parallel", "arbitrary"),
        name="ffn",
    )(x1, gf.reshape(1, d), w_gu_bf, w_gu_bf, w_dn_bf, gl.reshape(1, d))


def _tile(n, pref):
    t = min(pref, n)
    assert n % t == 0, (n, t)
    return t


def _tiles(seq, d_ff):
    return dict(
        proj=_tile(seq, 512),
        local=_tile(seq, 1024),
        global_q=_tile(seq, 512),
        global_kv=_tile(seq, 4096),
        ffn=_tile(seq, 1024),
        ffn_cols=_tile(d_ff, 512),
    )


def _layer(x, mem, rope, norm_mix_g, norm_mem_g, w_in, w_mem_kv, sink_a, q_norm_b_g, k_norm_b_g,
           out_norm_g, w_out, norm_ffn_g, w_gate_up, w_down, final_g):
    b, seq, d = x.shape
    tokens = b * seq
    assert seq % GRID_W == 0 and seq % WINDOW == 0
    x2 = x.reshape(tokens, d)
    t = _tiles(seq, w_down.shape[0])

    qa, ka, va, qb, kb, vbt, qm = _inproj(x2, seq, norm_mix_g, w_in.astype(BF16), q_norm_b_g, k_norm_b_g,
                                          rope, t["proj"], t["global_kv"])
    km, vm = _memkv(mem, norm_mem_g, w_mem_kv.astype(BF16))

    r3 = lambda a: a.reshape(b, seq, a.shape[-1])
    oa, om = _local_attention(r3(qa), r3(ka), r3(va), r3(qm), km, vm, sink_a,
                              out_norm_g[:A_Q], out_norm_g[A_Q + B_Q:], t["local"])
    vbt = vbt.reshape(b, seq // t["global_kv"], B_KV, t["global_kv"])
    ob = _global_attention(r3(qb), r3(kb), vbt, out_norm_g[A_Q:A_Q + B_Q],
                           _logits_bounded(q_norm_b_g, k_norm_b_g), t["global_q"])

    x1 = _outproj(oa.reshape(tokens, A_Q), ob.reshape(tokens, B_Q), om.reshape(tokens, M_Q), x2,
                  w_out.astype(BF16), t["proj"])
    y = _ffn(x1, norm_ffn_g, w_gate_up.astype(BF16), w_down.astype(BF16), final_g, t["ffn"], t["ffn_cols"])
    return y.reshape(b, seq, d)


def kernel(x_prompt, x_sample, mem_prompt, mem_sample, norm_mix_g, norm_mem_g, w_in, w_mem_kv, sink_a,
           q_norm_b_g, k_norm_b_g, out_norm_g, w_out, norm_ffn_g, w_gate_up, w_down, norm_final_g):
    depth = w_in.shape[0]
    assert depth == 1, "final norm is fused into the last layer's ffn kernel"
    args = (norm_mix_g[0], norm_mem_g[0], w_in[0], w_mem_kv[0], sink_a[0], q_norm_b_g[0], k_norm_b_g[0],
            out_norm_g[0], w_out[0], norm_ffn_g[0], w_gate_up[0], w_down[0], norm_final_g)
    rope = _rope_tables(max(x_prompt.shape[1], x_sample.shape[1]))
    return (_layer(x_prompt, mem_prompt, rope, *args), _layer(x_sample, mem_sample, rope, *args))
```

```python
import functools

import jax
import jax.numpy as jnp
from jax import lax
from jax.experimental import pallas as pl
from jax.experimental.pallas import tpu as pltpu

HEAD_DIM = 128
A_HEADS, A_KV_HEADS = 8, 2
B_HEADS, B_KV_HEADS = 4, 2
M_HEADS = 4
WINDOW = 128
GRID_W = 64
ROPE_THETA = 10000.0
NORM_EPS = 1e-6
NEG_INF = -1e30
A_Q, A_KV = A_HEADS * HEAD_DIM, A_KV_HEADS * HEAD_DIM
B_Q, B_KV = B_HEADS * HEAD_DIM, B_KV_HEADS * HEAD_DIM
M_Q = M_HEADS * HEAD_DIM
LOG2E = 1.4426950408889634
Q_SCALE = HEAD_DIM ** -0.5 * LOG2E
SOFTMAX_BOUND_LIMIT = 32.0
ROUNDING_SLACK = 1.02
KV_UNROLL = 2
ROW_BLOCK = 256

BF16 = jnp.bfloat16
F32 = jnp.float32
F32_SUBLANES = 8

VMEM_LIMIT_BYTES = 56 * 1024 * 1024


def _params(*sem):
    return pltpu.CompilerParams(dimension_semantics=sem, vmem_limit_bytes=VMEM_LIMIT_BYTES)


def _resident(shape, index_map):
    return pl.BlockSpec(shape, index_map, pipeline_mode=pl.Buffered(1))


def _rms(x, g):
    ms = jnp.mean(x * x, axis=-1, keepdims=True)
    return x * lax.rsqrt(ms + NORM_EPS) * g


def _dot(a, b):
    return jnp.dot(a, b, preferred_element_type=F32)


def _dot_nt(a, b):
    return lax.dot_general(a, b, (((1,), (1,)), ((), ())), preferred_element_type=F32)


def _inproj_kernel(x_ref, g_ref, w_ref, at_ref, ao_ref, br_ref, bc_ref, qg_ref, kg_ref,
                   qa_ref, ka_ref, va_ref, qb_ref, kb_ref, vbt_ref, qm_ref):
    h = _rms(x_ref[...], g_ref[...]).astype(BF16)
    tm = x_ref.shape[0]
    cos_t, sin_t, scos_t, ssin_t = (at_ref[i:i + 1, :] for i in range(4))
    cos_o, sin_o = ao_ref[0], ao_ref[1]
    ca = cos_t * cos_o - sin_t * sin_o
    sa = ssin_t * cos_o + scos_t * sin_o
    n_r = tm // GRID_W

    def axial(k):
        per_row = jnp.broadcast_to(br_ref[k][:, None, :], (n_r, GRID_W, HEAD_DIM)).reshape(tm, HEAD_DIM)
        return per_row + jnp.concatenate([bc_ref[k]] * n_r, axis=0)

    cb, sb = axial(0), axial(1)
    lane = lax.broadcasted_iota(jnp.int32, ca.shape, 1)
    low_quarter = (lane % (HEAD_DIM // 2)) < (HEAD_DIM // 4)

    def rope_a(t):
        return t * ca + pltpu.roll(t, HEAD_DIM // 2, 1) * sa

    def rope_b(t):
        partner = jnp.where(low_quarter, pltpu.roll(t, HEAD_DIM - HEAD_DIM // 4, 1),
                            pltpu.roll(t, HEAD_DIM // 4, 1))
        return t * cb + partner * sb

    def heads(t, n, fn, out_ref):
        for i in range(n):
            sl = slice(i * HEAD_DIM, (i + 1) * HEAD_DIM)
            out_ref[:, sl] = fn(t[:, sl]).astype(out_ref.dtype)

    widths = (A_Q, A_KV, A_KV, B_Q, B_KV, B_KV, M_Q)
    starts = [sum(widths[:i]) for i in range(len(widths))]

    def section(i):
        return _dot(h, w_ref[:, starts[i]:starts[i] + widths[i]])

    qg, kg = qg_ref[...], kg_ref[...]
    heads(section(3), B_HEADS, lambda t: rope_b(_rms(t, qg)) * Q_SCALE, qb_ref)
    heads(section(4), B_KV_HEADS, lambda t: rope_b(_rms(t, kg)), kb_ref)
    vbt_ref[...] = section(5).astype(vbt_ref.dtype).T
    heads(section(0), A_HEADS, lambda t: rope_a(t) * Q_SCALE, qa_ref)
    heads(section(1), A_KV_HEADS, rope_a, ka_ref)
    qm_ref[...] = (section(6) * Q_SCALE).astype(qm_ref.dtype)
    va_ref[...] = section(2).astype(va_ref.dtype)


def _rope_tables(seq, tm):
    def angles(n, dim):
        inv = ROPE_THETA ** (-(jnp.arange(0, dim, 2, dtype=F32) / dim))
        return jnp.arange(n, dtype=F32)[:, None] * jnp.tile(inv, 2)[None, :]

    def sign(dim):
        return jnp.where(jnp.arange(dim) < dim // 2, -1.0, 1.0).astype(F32)

    assert seq % tm == 0 and tm % (F32_SUBLANES * GRID_W) == 0
    start = angles(seq // tm, HEAD_DIM) * tm
    sg = sign(HEAD_DIM)
    a_tile = jnp.stack([jnp.cos(start), jnp.sin(start), sg * jnp.cos(start), sg * jnp.sin(start)], axis=1)
    a_tile = jnp.pad(a_tile, ((0, 0), (0, F32_SUBLANES - a_tile.shape[1]), (0, 0)))
    off = angles(tm, HEAD_DIM)
    a_off = jnp.stack([jnp.cos(off), jnp.sin(off)])

    half = HEAD_DIM // 2
    zeros = lambda n: jnp.zeros((n, half), F32)
    rows, cols = angles(seq // GRID_W, half), angles(GRID_W, half)
    b_row = jnp.stack([jnp.concatenate([jnp.cos(rows), zeros(rows.shape[0])], axis=1),
                       jnp.concatenate([jnp.sin(rows) * sign(half), zeros(rows.shape[0])], axis=1)])
    b_col = jnp.stack([jnp.concatenate([zeros(GRID_W), jnp.cos(cols)], axis=1),
                       jnp.concatenate([zeros(GRID_W), jnp.sin(cols) * sign(half)], axis=1)])
    return a_tile, a_off, b_row, b_col


def _inproj(x2, seq, g, w_bf, qg, kg, rope, tm, kv_block):
    tokens, d = x2.shape
    per_seq = seq // tm
    per_kv = kv_block // tm
    assert kv_block % tm == 0 and seq % kv_block == 0
    a_tile, a_off, b_row, b_col = rope
    assert a_off.shape[1] == tm
    row = lambda i: (i, 0)
    fixed = lambda i: (0, 0)
    widths = (A_Q, A_KV, A_KV, B_Q, B_KV, B_KV, M_Q)
    out_specs = [pl.BlockSpec((tm, w), row) for w in widths]
    out_shape = [jax.ShapeDtypeStruct((tokens, w), BF16) for w in widths]
    out_specs[5] = pl.BlockSpec((None, B_KV, tm), lambda i: (i // per_kv, 0, i % per_kv))
    out_shape[5] = jax.ShapeDtypeStruct((tokens // kv_block, B_KV, kv_block), BF16)
    return pl.pallas_call(
        _inproj_kernel,
        grid=(tokens // tm,),
        in_specs=[pl.BlockSpec((tm, d), row),
                  _resident((1, d), fixed),
                  _resident(w_bf.shape, fixed),
                  pl.BlockSpec((None, F32_SUBLANES, HEAD_DIM), lambda i: (i % per_seq, 0, 0)),
                  _resident(a_off.shape, lambda i: (0, 0, 0)),
                  pl.BlockSpec((2, tm // GRID_W, HEAD_DIM), lambda i: (0, i % per_seq, 0)),
                  _resident(b_col.shape, lambda i: (0, 0, 0)),
                  _resident((1, HEAD_DIM), fixed), _resident((1, HEAD_DIM), fixed)],
        out_specs=out_specs,
        out_shape=out_shape,
        compiler_params=_params("parallel"),
        name="inproj",
    )(x2, g.reshape(1, d), w_bf, a_tile, a_off, b_row, b_col, qg.reshape(1, HEAD_DIM), kg.reshape(1, HEAD_DIM))


def _memkv_kernel(mem_ref, g_ref, w_ref, km_ref, vm_ref):
    h = _rms(mem_ref[...], g_ref[...]).astype(BF16)
    kv = _dot(h, w_ref[...])
    km_ref[...] = kv[:, :M_Q].astype(km_ref.dtype)
    vm_ref[...] = kv[:, M_Q:].astype(vm_ref.dtype)


def _memkv(mem, g, w_bf):
    b, n, d = mem.shape
    blk = lambda i: (i, 0, 0)
    fixed = lambda i: (0, 0)
    return pl.pallas_call(
        _memkv_kernel,
        grid=(b,),
        in_specs=[pl.BlockSpec((None, n, d), blk), _resident((1, d), fixed), _resident(w_bf.shape, fixed)],
        out_specs=[pl.BlockSpec((None, n, M_Q), blk), pl.BlockSpec((None, n, M_Q), blk)],
        out_shape=[jax.ShapeDtypeStruct((b, n, M_Q), BF16)] * 2,
        compiler_params=_params("parallel"),
        name="memkv",
    )(mem, g.reshape(1, d), w_bf)


def _local_kernel(sink_ref, qa_ref, kp_ref, kc_ref, kn_ref, vp_ref, vc_ref, vn_ref,
                  qm_ref, km_ref, vm_ref, ga_ref, gm_ref, oa_ref, om_ref,
                  kbuf, vbuf, vmbuf, sbuf, smbuf, obuf, mbuf):
    bq = qa_ref.shape[0]
    nsub = bq // WINDOW
    i = pl.program_id(1)
    first = i == 0
    last = i == pl.num_programs(1) - 1
    group = A_HEADS // A_KV_HEADS
    cols = 3 * WINDOW
    two = 2 * HEAD_DIM

    kbuf[0:WINDOW] = kp_ref[...]
    kbuf[WINDOW:WINDOW + bq] = kc_ref[...]
    kbuf[WINDOW + bq:] = kn_ref[...]
    for g in range(A_KV_HEADS):
        sl = slice(g * HEAD_DIM, (g + 1) * HEAD_DIM)
        vbuf[0:WINDOW, g * two:g * two + HEAD_DIM] = vp_ref[:, sl]
        vbuf[WINDOW:WINDOW + bq, g * two:g * two + HEAD_DIM] = vc_ref[:, sl]
        vbuf[WINDOW + bq:, g * two:g * two + HEAD_DIM] = vn_ref[:, sl]
        vbuf[:, g * two + HEAD_DIM:(g + 1) * two] = jnp.ones((bq + 2 * WINDOW, HEAD_DIM), vbuf.dtype)
    for h in range(M_HEADS):
        vmbuf[:, h * two:h * two + HEAD_DIM] = vm_ref[:, h * HEAD_DIM:(h + 1) * HEAD_DIM]
        vmbuf[:, h * two + HEAD_DIM:(h + 1) * two] = jnp.ones((vm_ref.shape[0], HEAD_DIM), vmbuf.dtype)

    for j in range(nsub):
        for g in range(A_KV_HEADS):
            q = jnp.concatenate(
                [qa_ref[j * WINDOW:(j + 1) * WINDOW, (g * group + h) * HEAD_DIM:(g * group + h + 1) * HEAD_DIM]
                 for h in range(group)], axis=0)
            sbuf[j * A_KV_HEADS + g] = _dot_nt(q, kbuf[j * WINDOW:j * WINDOW + cols, g * HEAD_DIM:(g + 1) * HEAD_DIM])
    for h in range(M_HEADS):
        sl = slice(h * HEAD_DIM, (h + 1) * HEAD_DIM)
        smbuf[h] = _dot_nt(qm_ref[:, sl], km_ref[:, sl])

    qi = lax.broadcasted_iota(jnp.int32, (WINDOW, WINDOW), 0)
    kc = lax.broadcasted_iota(jnp.int32, (WINDOW, WINDOW), 1)
    neg = jnp.float32(NEG_INF)
    tile_heads = lambda t: jnp.concatenate([t] * group, axis=0)
    bias_left = jnp.where(kc >= qi, 0.0, neg)
    bias_right = jnp.where(kc <= qi, 0.0, neg)
    bias_left_first = tile_heads(jnp.where(first, neg, bias_left))
    bias_right_last = tile_heads(jnp.where(last, neg, bias_right))
    bias_left, bias_right = tile_heads(bias_left), tile_heads(bias_right)

    def softmax_pv(chunks, m, v):
        p = jnp.concatenate([jnp.exp2(c - m) for c in chunks], axis=1).astype(BF16)
        pv = _dot(p, v)
        return pv[:, :HEAD_DIM], pv[:, HEAD_DIM:]

    for j in range(nsub):
        for g in range(A_KV_HEADS):
            s = sbuf[j * A_KV_HEADS + g]
            chunks = [s[:, 0:WINDOW] + (bias_left_first if j == 0 else bias_left),
                      s[:, WINDOW:2 * WINDOW],
                      s[:, 2 * WINDOW:] + (bias_right_last if j == nsub - 1 else bias_right)]
            sink = jnp.concatenate(
                [jnp.full((WINDOW, HEAD_DIM), sink_ref[g * group + h] * LOG2E, F32) for h in range(group)], axis=0)
            m = jnp.maximum(jnp.max(functools.reduce(jnp.maximum, chunks), axis=-1, keepdims=True), sink)
            num, den = softmax_pv(chunks, m, vbuf[j * WINDOW:j * WINDOW + cols, g * two:(g + 1) * two])
            o = num * (1.0 / (den + jnp.exp2(sink - m)))
            for h in range(group):
                c0 = (g * group + h) * HEAD_DIM
                obuf[j * WINDOW:(j + 1) * WINDOW, c0:c0 + HEAD_DIM] = o[h * WINDOW:(h + 1) * WINDOW]
    oa_ref[...] = _rms(obuf[...], ga_ref[...]).astype(oa_ref.dtype)

    for h in range(M_HEADS):
        s = smbuf[h]
        chunks = [s[:, c * HEAD_DIM:(c + 1) * HEAD_DIM] for c in range(s.shape[1] // HEAD_DIM)]
        m = jnp.max(functools.reduce(jnp.maximum, chunks), axis=-1, keepdims=True)
        m = jnp.broadcast_to(m, chunks[0].shape)
        num, den = softmax_pv(chunks, m, vmbuf[:, h * two:(h + 1) * two])
        mbuf[:, h * HEAD_DIM:(h + 1) * HEAD_DIM] = num * (1.0 / den)
    om_ref[...] = _rms(mbuf[...], gm_ref[...]).astype(om_ref.dtype)


def _local_attention(qa, ka, va, qm, km, vm, sink, ga, gm, bq):
    b, seq, _ = qa.shape
    nt = seq // bq
    sub = bq // WINDOW
    nblk = seq // WINDOW
    n_mem = km.shape[1]
    tile = lambda bi, i: (bi, i, 0)
    prev = lambda bi, i: (bi, jnp.maximum(i * sub - 1, 0), 0)
    nxt = lambda bi, i: (bi, jnp.minimum((i + 1) * sub, nblk - 1), 0)
    per_b = lambda bi, i: (bi, 0, 0)
    fixed = lambda bi, i: (0, 0)
    kv_specs = [pl.BlockSpec((None, WINDOW, A_KV), prev), pl.BlockSpec((None, bq, A_KV), tile),
                pl.BlockSpec((None, WINDOW, A_KV), nxt)]
    return pl.pallas_call(
        _local_kernel,
        grid=(b, nt),
        in_specs=[pl.BlockSpec(memory_space=pltpu.SMEM),
                  pl.BlockSpec((None, bq, A_Q), tile)] + kv_specs + kv_specs + [
                  pl.BlockSpec((None, bq, M_Q), tile),
                  pl.BlockSpec((None, n_mem, M_Q), per_b), pl.BlockSpec((None, n_mem, M_Q), per_b),
                  _resident((1, A_Q), fixed), _resident((1, M_Q), fixed)],
        out_specs=[pl.BlockSpec((None, bq, A_Q), tile), pl.BlockSpec((None, bq, M_Q), tile)],
        out_shape=[jax.ShapeDtypeStruct((b, seq, A_Q), BF16), jax.ShapeDtypeStruct((b, seq, M_Q), BF16)],
        scratch_shapes=[pltpu.VMEM((bq + 2 * WINDOW, A_KV), BF16),
                        pltpu.VMEM((bq + 2 * WINDOW, 2 * A_KV), BF16),
                        pltpu.VMEM((n_mem, 2 * M_Q), BF16),
                        pltpu.VMEM((sub * A_KV_HEADS, (A_HEADS // A_KV_HEADS) * WINDOW, 3 * WINDOW), F32),
                        pltpu.VMEM((M_HEADS, bq, n_mem), F32),
                        pltpu.VMEM((bq, A_Q), F32), pltpu.VMEM((bq, M_Q), F32)],
        compiler_params=_params("parallel", "parallel"),
        name="local_attn",
    )(sink, qa, ka, ka, ka, va, va, va, qm, km, vm, ga.reshape(1, A_Q), gm.reshape(1, M_Q))


def _global_heads(q_ref, k_ref, vt_ref, obuf, bounded):
    bq = q_ref.shape[0]
    n_kv, _, bk = vt_ref.shape
    group = B_HEADS // B_KV_HEADS
    rows = group * bq
    sub = F32_SUBLANES
    l0 = jnp.zeros((sub, rows), F32)
    acc0 = jnp.zeros((HEAD_DIM, rows), F32)
    kv_heads = range(B_KV_HEADS)
    ksl = [slice(g * HEAD_DIM, (g + 1) * HEAD_DIM) for g in kv_heads]
    qs = [jnp.concatenate([q_ref[:, (g * group + h) * HEAD_DIM:(g * group + h + 1) * HEAD_DIM]
                           for h in range(group)], axis=0) for g in kv_heads]

    def logits_t(g, j):
        start = pl.multiple_of(j * bk, bk)
        return _dot_nt(k_ref[pl.ds(start, bk), ksl[g]], qs[g]).reshape(bk // sub, sub, rows)

    def pv_t(g, j, pt):
        return _dot(vt_ref[j, ksl[g], :], pt.reshape(bk, rows).astype(BF16))

    def bounded_body(j, carry):
        out = []
        for g, (l, acc) in zip(kv_heads, carry):
            pt = jnp.exp2(logits_t(g, j))
            out.append((l + jnp.sum(pt, axis=0), acc + pv_t(g, j, pt)))
        return tuple(out)

    def online_body(j, carry):
        out = []
        for g, (m, l, acc) in zip(kv_heads, carry):
            st = logits_t(g, j)
            m_new = jnp.maximum(m, jnp.max(jnp.max(st, axis=0), axis=0, keepdims=True))
            alpha = jnp.exp2(m - m_new)
            pt = jnp.exp2(st - m_new[None])
            out.append((m_new, alpha * l + jnp.sum(pt, axis=0), alpha[:1] * acc + pv_t(g, j, pt)))
        return tuple(out)

    if bounded:
        res = lax.fori_loop(0, n_kv, bounded_body, ((l0, acc0),) * B_KV_HEADS, unroll=min(n_kv, KV_UNROLL))
    else:
        m0 = jnp.full((sub, rows), NEG_INF, F32)
        res = [r[1:] for r in lax.fori_loop(0, n_kv, online_body, ((m0, l0, acc0),) * B_KV_HEADS)]
    for g, (l, acc) in zip(kv_heads, res):
        o = (acc * (1.0 / jnp.sum(l, axis=0, keepdims=True))).T
        for h in range(group):
            c0 = (g * group + h) * HEAD_DIM
            obuf[:, c0:c0 + HEAD_DIM] = o[h * bq:(h + 1) * bq]


def _global_kernel(q_ref, k_ref, vt_ref, g_ref, o_ref, obuf, *, bounded):
    _global_heads(q_ref, k_ref, vt_ref, obuf, bounded)
    o_ref[...] = _rms(obuf[...], g_ref[...]).astype(o_ref.dtype)


def _logits_bounded(qg, kg):
    bound = (HEAD_DIM ** 0.5) * jnp.max(jnp.abs(qg)) * jnp.max(jnp.abs(kg)) * ROUNDING_SLACK
    return bound <= SOFTMAX_BOUND_LIMIT


def _global_attention(q, k, vt, gb, bounded, bq):
    b, seq, _ = q.shape
    n_kv, _, bk = vt.shape[1:]
    tile = lambda bi, i: (bi, i, 0)
    fixed = lambda bi, i: (0, 0)

    def call(is_bounded):
        return pl.pallas_call(
            functools.partial(_global_kernel, bounded=is_bounded),
            grid=(b, seq // bq),
            in_specs=[pl.BlockSpec((None, bq, B_Q), tile),
                      _resident((None, seq, B_KV), lambda bi, i: (bi, 0, 0)),
                      _resident((None, n_kv, B_KV, bk), lambda bi, i: (bi, 0, 0, 0)),
                      _resident((1, B_Q), fixed)],
            out_specs=pl.BlockSpec((None, bq, B_Q), tile),
            out_shape=jax.ShapeDtypeStruct((b, seq, B_Q), BF16),
            scratch_shapes=[pltpu.VMEM((bq, B_Q), F32)],
            compiler_params=_params("parallel", "parallel"),
            name="global_attn_bounded" if is_bounded else "global_attn_online",
        )

    return lax.cond(bounded, call(True), call(False), q, k, vt, gb.reshape(1, B_Q))


def _outproj_kernel(oa_ref, ob_ref, om_ref, x_ref, w_ref, x1_ref):
    merged = jnp.concatenate([oa_ref[...], ob_ref[...], om_ref[...]], axis=1)
    x1_ref[...] = x_ref[...] + _dot(merged, w_ref[...])


def _outproj(oa, ob, om, x2, w_bf, tm):
    tokens, d = x2.shape
    row = lambda i: (i, 0)
    fixed = lambda i: (0, 0)
    return pl.pallas_call(
        _outproj_kernel,
        grid=(tokens // tm,),
        in_specs=[pl.BlockSpec((tm, A_Q), row), pl.BlockSpec((tm, B_Q), row), pl.BlockSpec((tm, M_Q), row),
                  pl.BlockSpec((tm, d), row), _resident(w_bf.shape, fixed)],
        out_specs=pl.BlockSpec((tm, d), row),
        out_shape=jax.ShapeDtypeStruct((tokens, d), F32),
        compiler_params=_params("parallel"),
        name="outproj",
    )(oa, ob, om, x2, w_bf)


def _ffn_kernel(x1_ref, gf_ref, wg_ref, wu_ref, wd_ref, gl_ref, o_ref, h_scr):
    j = pl.program_id(1)
    last = pl.num_programs(1) - 1
    n_blocks = x1_ref.shape[0] // ROW_BLOCK
    rows = lambda r: slice(r * ROW_BLOCK, (r + 1) * ROW_BLOCK)

    def chunk(first, final):
        def gate_up(r):
            if first:
                h_scr[rows(r)] = _rms(x1_ref[rows(r)], gf_ref[...]).astype(h_scr.dtype)
            h = h_scr[rows(r)]
            return _dot(h, wg_ref[...]), _dot(h, wu_ref[...])

        pending = gate_up(0)
        for r in range(n_blocks):
            ahead = gate_up(r + 1) if r + 1 < n_blocks else None
            gate, up = pending
            act = (gate * jax.nn.sigmoid(gate) * up).astype(BF16)
            acc = (x1_ref[rows(r)] if first else o_ref[rows(r)]) + _dot(act, wd_ref[...])
            o_ref[rows(r)] = _rms(acc, gl_ref[...]) if final else acc
            pending = ahead

    pl.when(j == 0)(lambda: chunk(True, False))
    pl.when(jnp.logical_and(j > 0, j < last))(lambda: chunk(False, False))
    pl.when(j == last)(lambda: chunk(False, True))


def _ffn(x1, gf, w_gu_bf, w_dn_bf, gl, tm, tf):
    tokens, d = x1.shape
    d_ff = w_dn_bf.shape[0]
    nf = d_ff // tf
    assert nf >= 2 and tm % ROW_BLOCK == 0
    row = lambda i, j: (i, 0)
    fixed = lambda i, j: (0, 0)
    return pl.pallas_call(
        _ffn_kernel,
        grid=(tokens // tm, nf),
        in_specs=[pl.BlockSpec((tm, d), row), _resident((1, d), fixed),
                  pl.BlockSpec((d, tf), lambda i, j: (0, j)), pl.BlockSpec((d, tf), lambda i, j: (0, j + nf)),
                  pl.BlockSpec((tf, d), lambda i, j: (j, 0)), _resident((1, d), fixed)],
        out_specs=pl.BlockSpec((tm, d), row),
        out_shape=jax.ShapeDtypeStruct((tokens, d), F32),
        scratch_shapes=[pltpu.VMEM((tm, d), BF16)],
        compiler_params=_params("parallel", "arbitrary"),
        name="ffn",
    )(x1, gf.reshape(1, d), w_gu_bf, w_gu_bf, w_dn_bf, gl.reshape(1, d))


def _tile(n, pref):
    t = min(pref, n)
    assert n % t == 0, (n, t)
    return t


def _tiles(seq, d_ff):
    return dict(
        proj=_tile(seq, 512),
        local=_tile(seq, 1024),
        global_q=_tile(seq, 512),
        global_kv=_tile(seq, 4096),
        ffn=_tile(seq, 1024),
        ffn_cols=_tile(d_ff, 512),
    )


def _layer(x, mem, rope, norm_mix_g, norm_mem_g, w_in, w_mem_kv, sink_a, q_norm_b_g, k_norm_b_g,
           out_norm_g, w_out, norm_ffn_g, w_gate_up, w_down, final_g):
    b, seq, d = x.shape
    tokens = b * seq
    assert seq % GRID_W == 0 and seq % WINDOW == 0
    x2 = x.reshape(tokens, d)
    t = _tiles(seq, w_down.shape[0])

    qa, ka, va, qb, kb, vbt, qm = _inproj(x2, seq, norm_mix_g, w_in.astype(BF16), q_norm_b_g, k_norm_b_g,
                                          rope, t["proj"], t["global_kv"])
    km, vm = _memkv(mem, norm_mem_g, w_mem_kv.astype(BF16))

    r3 = lambda a: a.reshape(b, seq, a.shape[-1])
    oa, om = _local_attention(r3(qa), r3(ka), r3(va), r3(qm), km, vm, sink_a,
                              out_norm_g[:A_Q], out_norm_g[A_Q + B_Q:], t["local"])
    vbt = vbt.reshape(b, seq // t["global_kv"], B_KV, t["global_kv"])
    ob = _global_attention(r3(qb), r3(kb), vbt, out_norm_g[A_Q:A_Q + B_Q],
                           _logits_bounded(q_norm_b_g, k_norm_b_g), t["global_q"])

    x1 = _outproj(oa.reshape(tokens, A_Q), ob.reshape(tokens, B_Q), om.reshape(tokens, M_Q), x2,
                  w_out.astype(BF16), t["proj"])
    y = _ffn(x1, norm_ffn_g, w_gate_up.astype(BF16), w_down.astype(BF16), final_g, t["ffn"], t["ffn_cols"])
    return y.reshape(b, seq, d)


def kernel(x_prompt, x_sample, mem_prompt, mem_sample, norm_mix_g, norm_mem_g, w_in, w_mem_kv, sink_a,
           q_norm_b_g, k_norm_b_g, out_norm_g, w_out, norm_ffn_g, w_gate_up, w_down, norm_final_g):
    depth = w_in.shape[0]
    assert depth == 1, "final norm is fused into the last layer's ffn kernel"
    args = (norm_mix_g[0], norm_mem_g[0], w_in[0], w_mem_kv[0], sink_a[0], q_norm_b_g[0], k_norm_b_g[0],
            out_norm_g[0], w_out[0], norm_ffn_g[0], w_gate_up[0], w_down[0], norm_final_g)
    seq_max = max(x_prompt.shape[1], x_sample.shape[1])
    tm = _tiles(seq_max, w_down.shape[1])["proj"]
    assert _tiles(min(x_prompt.shape[1], x_sample.shape[1]), w_down.shape[1])["proj"] == tm
    rope = _rope_tables(seq_max, tm)
    return (_layer(x_prompt, mem_prompt, rope, *args), _layer(x_sample, mem_sample, rope, *args))
```

```python
import functools

import jax
import jax.numpy as jnp
from jax import lax
from jax.experimental import pallas as pl
from jax.experimental.pallas import tpu as pltpu

HEAD_DIM = 128
A_HEADS, A_KV_HEADS = 8, 2
B_HEADS, B_KV_HEADS = 4, 2
M_HEADS = 4
WINDOW = 128
GRID_W = 64
ROPE_THETA = 10000.0
NORM_EPS = 1e-6
NEG_INF = -1e30
A_Q, A_KV = A_HEADS * HEAD_DIM, A_KV_HEADS * HEAD_DIM
B_Q, B_KV = B_HEADS * HEAD_DIM, B_KV_HEADS * HEAD_DIM
M_Q = M_HEADS * HEAD_DIM
LOG2E = 1.4426950408889634
Q_SCALE = HEAD_DIM ** -0.5 * LOG2E
SOFTMAX_BOUND_LIMIT = 32.0
ROUNDING_SLACK = 1.02
KV_UNROLL = 2
ROW_BLOCK = 256

BF16 = jnp.bfloat16
F32 = jnp.float32
F32_SUBLANES = 8

VMEM_LIMIT_BYTES = 56 * 1024 * 1024


def _params(*sem):
    return pltpu.CompilerParams(dimension_semantics=sem, vmem_limit_bytes=VMEM_LIMIT_BYTES)


def _resident(shape, index_map):
    return pl.BlockSpec(shape, index_map, pipeline_mode=pl.Buffered(1))


def _rms(x, g):
    ms = jnp.mean(x * x, axis=-1, keepdims=True)
    return x * lax.rsqrt(ms + NORM_EPS) * g


def _dot(a, b):
    return jnp.dot(a, b, preferred_element_type=F32)


def _dot_nt(a, b):
    return lax.dot_general(a, b, (((1,), (1,)), ((), ())), preferred_element_type=F32)


def _inproj_kernel(x_ref, g_ref, w_ref, at_ref, ao_ref, br_ref, bc_ref, qg_ref, kg_ref,
                   qa_ref, ka_ref, va_ref, qb_ref, kb_ref, vbt_ref, qm_ref):
    h = _rms(x_ref[...], g_ref[...]).astype(BF16)
    tm = x_ref.shape[0]
    cos_t, sin_t, scos_t, ssin_t = (at_ref[i:i + 1, :] for i in range(4))
    cos_o, sin_o = ao_ref[0], ao_ref[1]
    ca = cos_t * cos_o - sin_t * sin_o
    sa = ssin_t * cos_o + scos_t * sin_o
    n_r = tm // GRID_W

    def axial(k):
        per_row = jnp.broadcast_to(br_ref[k][:, None, :], (n_r, GRID_W, HEAD_DIM)).reshape(tm, HEAD_DIM)
        return per_row + jnp.concatenate([bc_ref[k]] * n_r, axis=0)

    cb, sb = axial(0), axial(1)
    lane = lax.broadcasted_iota(jnp.int32, ca.shape, 1)
    low_quarter = (lane % (HEAD_DIM // 2)) < (HEAD_DIM // 4)

    def rope_a(t):
        return t * ca + pltpu.roll(t, HEAD_DIM // 2, 1) * sa

    def rope_b(t):
        partner = jnp.where(low_quarter, pltpu.roll(t, HEAD_DIM - HEAD_DIM // 4, 1),
                            pltpu.roll(t, HEAD_DIM // 4, 1))
        return t * cb + partner * sb

    def heads(t, n, fn, out_ref):
        for i in range(n):
            sl = slice(i * HEAD_DIM, (i + 1) * HEAD_DIM)
            out_ref[:, sl] = fn(t[:, sl]).astype(out_ref.dtype)

    widths = (A_Q, A_KV, A_KV, B_Q, B_KV, B_KV, M_Q)
    starts = [sum(widths[:i]) for i in range(len(widths))]

    def section(i):
        return _dot(h, w_ref[:, starts[i]:starts[i] + widths[i]])

    qg, kg = qg_ref[...], kg_ref[...]
    heads(section(3), B_HEADS, lambda t: rope_b(_rms(t, qg)) * Q_SCALE, qb_ref)
    heads(section(4), B_KV_HEADS, lambda t: rope_b(_rms(t, kg)), kb_ref)
    vbt_ref[...] = section(5).astype(vbt_ref.dtype).T
    heads(section(0), A_HEADS, lambda t: rope_a(t) * Q_SCALE, qa_ref)
    heads(section(1), A_KV_HEADS, rope_a, ka_ref)
    qm_ref[...] = (section(6) * Q_SCALE).astype(qm_ref.dtype)
    va_ref[...] = section(2).astype(va_ref.dtype)


def _rope_tables(seq, tm):
    def angles(n, dim):
        inv = ROPE_THETA ** (-(jnp.arange(0, dim, 2, dtype=F32) / dim))
        return jnp.arange(n, dtype=F32)[:, None] * jnp.tile(inv, 2)[None, :]

    def sign(dim):
        return jnp.where(jnp.arange(dim) < dim // 2, -1.0, 1.0).astype(F32)

    assert seq % tm == 0 and tm % (F32_SUBLANES * GRID_W) == 0
    start = angles(seq // tm, HEAD_DIM) * tm
    sg = sign(HEAD_DIM)
    a_tile = jnp.stack([jnp.cos(start), jnp.sin(start), sg * jnp.cos(start), sg * jnp.sin(start)], axis=1)
    a_tile = jnp.pad(a_tile, ((0, 0), (0, F32_SUBLANES - a_tile.shape[1]), (0, 0)))
    off = angles(tm, HEAD_DIM)
    a_off = jnp.stack([jnp.cos(off), jnp.sin(off)])

    half = HEAD_DIM // 2
    zeros = lambda n: jnp.zeros((n, half), F32)
    rows, cols = angles(seq // GRID_W, half), angles(GRID_W, half)
    b_row = jnp.stack([jnp.concatenate([jnp.cos(rows), zeros(rows.shape[0])], axis=1),
                       jnp.concatenate([jnp.sin(rows) * sign(half), zeros(rows.shape[0])], axis=1)])
    b_col = jnp.stack([jnp.concatenate([zeros(GRID_W), jnp.cos(cols)], axis=1),
                       jnp.concatenate([zeros(GRID_W), jnp.sin(cols) * sign(half)], axis=1)])
    return a_tile, a_off, b_row, b_col


def _inproj(x2, seq, g, w_bf, qg, kg, rope, tm, kv_block):
    tokens, d = x2.shape
    per_seq = seq // tm
    per_kv = kv_block // tm
    assert kv_block % tm == 0 and seq % kv_block == 0
    a_tile, a_off, b_row, b_col = rope
    assert a_off.shape[1] == tm
    row = lambda i: (i, 0)
    fixed = lambda i: (0, 0)
    widths = (A_Q, A_KV, A_KV, B_Q, B_KV, B_KV, M_Q)
    out_specs = [pl.BlockSpec((tm, w), row) for w in widths]
    out_shape = [jax.ShapeDtypeStruct((tokens, w), BF16) for w in widths]
    out_specs[5] = pl.BlockSpec((None, B_KV, tm), lambda i: (i // per_kv, 0, i % per_kv))
    out_shape[5] = jax.ShapeDtypeStruct((tokens // kv_block, B_KV, kv_block), BF16)
    return pl.pallas_call(
        _inproj_kernel,
        grid=(tokens // tm,),
        in_specs=[pl.BlockSpec((tm, d), row),
                  _resident((1, d), fixed),
                  _resident(w_bf.shape, fixed),
                  pl.BlockSpec((None, F32_SUBLANES, HEAD_DIM), lambda i: (i % per_seq, 0, 0)),
                  _resident(a_off.shape, lambda i: (0, 0, 0)),
                  pl.BlockSpec((2, tm // GRID_W, HEAD_DIM), lambda i: (0, i % per_seq, 0)),
                  _resident(b_col.shape, lambda i: (0, 0, 0)),
                  _resident((1, HEAD_DIM), fixed), _resident((1, HEAD_DIM), fixed)],
        out_specs=out_specs,
        out_shape=out_shape,
        compiler_params=_params("parallel"),
        name="inproj",
    )(x2, g.reshape(1, d), w_bf, a_tile, a_off, b_row, b_col, qg.reshape(1, HEAD_DIM), kg.reshape(1, HEAD_DIM))


def _memkv_kernel(mem_ref, g_ref, w_ref, km_ref, vm_ref):
    h = _rms(mem_ref[...], g_ref[...]).astype(BF16)
    kv = _dot(h, w_ref[...])
    km_ref[...] = kv[:, :M_Q].astype(km_ref.dtype)
    vm_ref[...] = kv[:, M_Q:].astype(vm_ref.dtype)


def _memkv(mem, g, w_bf):
    b, n, d = mem.shape
    blk = lambda i: (i, 0, 0)
    fixed = lambda i: (0, 0)
    return pl.pallas_call(
        _memkv_kernel,
        grid=(b,),
        in_specs=[pl.BlockSpec((None, n, d), blk), _resident((1, d), fixed), _resident(w_bf.shape, fixed)],
        out_specs=[pl.BlockSpec((None, n, M_Q), blk), pl.BlockSpec((None, n, M_Q), blk)],
        out_shape=[jax.ShapeDtypeStruct((b, n, M_Q), BF16)] * 2,
        compiler_params=_params("parallel"),
        name="memkv",
    )(mem, g.reshape(1, d), w_bf)


def _local_kernel(sink_ref, qa_ref, kp_ref, kc_ref, kn_ref, vp_ref, vc_ref, vn_ref,
                  qm_ref, km_ref, vm_ref, ga_ref, gm_ref, oa_ref, om_ref,
                  kbuf, vbuf, vmbuf, sbuf, smbuf, obuf, mbuf):
    bq = qa_ref.shape[0]
    nsub = bq // WINDOW
    i = pl.program_id(1)
    first = i == 0
    last = i == pl.num_programs(1) - 1
    group = A_HEADS // A_KV_HEADS
    cols = 3 * WINDOW
    two = 2 * HEAD_DIM

    kbuf[0:WINDOW] = kp_ref[...]
    kbuf[WINDOW:WINDOW + bq] = kc_ref[...]
    kbuf[WINDOW + bq:] = kn_ref[...]
    for g in range(A_KV_HEADS):
        sl = slice(g * HEAD_DIM, (g + 1) * HEAD_DIM)
        vbuf[0:WINDOW, g * two:g * two + HEAD_DIM] = vp_ref[:, sl]
        vbuf[WINDOW:WINDOW + bq, g * two:g * two + HEAD_DIM] = vc_ref[:, sl]
        vbuf[WINDOW + bq:, g * two:g * two + HEAD_DIM] = vn_ref[:, sl]
        vbuf[:, g * two + HEAD_DIM:(g + 1) * two] = jnp.ones((bq + 2 * WINDOW, HEAD_DIM), vbuf.dtype)
    for h in range(M_HEADS):
        vmbuf[:, h * two:h * two + HEAD_DIM] = vm_ref[:, h * HEAD_DIM:(h + 1) * HEAD_DIM]
        vmbuf[:, h * two + HEAD_DIM:(h + 1) * two] = jnp.ones((vm_ref.shape[0], HEAD_DIM), vmbuf.dtype)

    for j in range(nsub):
        for g in range(A_KV_HEADS):
            q = jnp.concatenate(
                [qa_ref[j * WINDOW:(j + 1) * WINDOW, (g * group + h) * HEAD_DIM:(g * group + h + 1) * HEAD_DIM]
                 for h in range(group)], axis=0)
            sbuf[j * A_KV_HEADS + g] = _dot_nt(q, kbuf[j * WINDOW:j * WINDOW + cols, g * HEAD_DIM:(g + 1) * HEAD_DIM])
    for h in range(M_HEADS):
        sl = slice(h * HEAD_DIM, (h + 1) * HEAD_DIM)
        smbuf[h] = _dot_nt(qm_ref[:, sl], km_ref[:, sl])

    qi = lax.broadcasted_iota(jnp.int32, (WINDOW, WINDOW), 0)
    kc = lax.broadcasted_iota(jnp.int32, (WINDOW, WINDOW), 1)
    neg = jnp.float32(NEG_INF)
    tile_heads = lambda t: jnp.concatenate([t] * group, axis=0)
    bias_left = jnp.where(kc >= qi, 0.0, neg)
    bias_right = jnp.where(kc <= qi, 0.0, neg)
    bias_left_first = tile_heads(jnp.where(first, neg, bias_left))
    bias_right_last = tile_heads(jnp.where(last, neg, bias_right))
    bias_left, bias_right = tile_heads(bias_left), tile_heads(bias_right)

    def softmax_pv(chunks, m, v):
        p = jnp.concatenate([jnp.exp2(c - m) for c in chunks], axis=1).astype(BF16)
        pv = _dot(p, v)
        return pv[:, :HEAD_DIM], pv[:, HEAD_DIM:]

    for j in range(nsub):
        for g in range(A_KV_HEADS):
            s = sbuf[j * A_KV_HEADS + g]
            chunks = [s[:, 0:WINDOW] + (bias_left_first if j == 0 else bias_left),
                      s[:, WINDOW:2 * WINDOW],
                      s[:, 2 * WINDOW:] + (bias_right_last if j == nsub - 1 else bias_right)]
            sink = jnp.concatenate(
                [jnp.full((WINDOW, HEAD_DIM), sink_ref[g * group + h] * LOG2E, F32) for h in range(group)], axis=0)
            m = jnp.maximum(jnp.max(functools.reduce(jnp.maximum, chunks), axis=-1, keepdims=True), sink)
            num, den = softmax_pv(chunks, m, vbuf[j * WINDOW:j * WINDOW + cols, g * two:(g + 1) * two])
            o = num * (1.0 / (den + jnp.exp2(sink - m)))
            for h in range(group):
                c0 = (g * group + h) * HEAD_DIM
                obuf[j * WINDOW:(j + 1) * WINDOW, c0:c0 + HEAD_DIM] = o[h * WINDOW:(h + 1) * WINDOW]
    oa_ref[...] = _rms(obuf[...], ga_ref[...]).astype(oa_ref.dtype)

    for h in range(M_HEADS):
        s = smbuf[h]
        chunks = [s[:, c * HEAD_DIM:(c + 1) * HEAD_DIM] for c in range(s.shape[1] // HEAD_DIM)]
        m = jnp.max(functools.reduce(jnp.maximum, chunks), axis=-1, keepdims=True)
        m = jnp.broadcast_to(m, chunks[0].shape)
        num, den = softmax_pv(chunks, m, vmbuf[:, h * two:(h + 1) * two])
        mbuf[:, h * HEAD_DIM:(h + 1) * HEAD_DIM] = num * (1.0 / den)
    om_ref[...] = _rms(mbuf[...], gm_ref[...]).astype(om_ref.dtype)


def _local_attention(qa, ka, va, qm, km, vm, sink, ga, gm, bq):
    b, seq, _ = qa.shape
    nt = seq // bq
    sub = bq // WINDOW
    nblk = seq // WINDOW
    n_mem = km.shape[1]
    tile = lambda bi, i: (bi, i, 0)
    prev = lambda bi, i: (bi, jnp.maximum(i * sub - 1, 0), 0)
    nxt = lambda bi, i: (bi, jnp.minimum((i + 1) * sub, nblk - 1), 0)
    per_b = lambda bi, i: (bi, 0, 0)
    fixed = lambda bi, i: (0, 0)
    kv_specs = [pl.BlockSpec((None, WINDOW, A_KV), prev), pl.BlockSpec((None, bq, A_KV), tile),
                pl.BlockSpec((None, WINDOW, A_KV), nxt)]
    return pl.pallas_call(
        _local_kernel,
        grid=(b, nt),
        in_specs=[pl.BlockSpec(memory_space=pltpu.SMEM),
                  pl.BlockSpec((None, bq, A_Q), tile)] + kv_specs + kv_specs + [
                  pl.BlockSpec((None, bq, M_Q), tile),
                  pl.BlockSpec((None, n_mem, M_Q), per_b), pl.BlockSpec((None, n_mem, M_Q), per_b),
                  _resident((1, A_Q), fixed), _resident((1, M_Q), fixed)],
        out_specs=[pl.BlockSpec((None, bq, A_Q), tile), pl.BlockSpec((None, bq, M_Q), tile)],
        out_shape=[jax.ShapeDtypeStruct((b, seq, A_Q), BF16), jax.ShapeDtypeStruct((b, seq, M_Q), BF16)],
        scratch_shapes=[pltpu.VMEM((bq + 2 * WINDOW, A_KV), BF16),
                        pltpu.VMEM((bq + 2 * WINDOW, 2 * A_KV), BF16),
                        pltpu.VMEM((n_mem, 2 * M_Q), BF16),
                        pltpu.VMEM((sub * A_KV_HEADS, (A_HEADS // A_KV_HEADS) * WINDOW, 3 * WINDOW), F32),
                        pltpu.VMEM((M_HEADS, bq, n_mem), F32),
                        pltpu.VMEM((bq, A_Q), F32), pltpu.VMEM((bq, M_Q), F32)],
        compiler_params=_params("parallel", "parallel"),
        name="local_attn",
    )(sink, qa, ka, ka, ka, va, va, va, qm, km, vm, ga.reshape(1, A_Q), gm.reshape(1, M_Q))


def _global_heads(q_ref, k_ref, vt_ref, obuf, bounded):
    bq = q_ref.shape[0]
    n_kv, _, bk = vt_ref.shape
    group = B_HEADS // B_KV_HEADS
    rows = group * bq
    sub = F32_SUBLANES
    l0 = jnp.zeros((sub, rows), F32)
    acc0 = jnp.zeros((HEAD_DIM, rows), F32)
    kv_heads = range(B_KV_HEADS)
    ksl = [slice(g * HEAD_DIM, (g + 1) * HEAD_DIM) for g in kv_heads]
    qs = [jnp.concatenate([q_ref[:, (g * group + h) * HEAD_DIM:(g * group + h + 1) * HEAD_DIM]
                           for h in range(group)], axis=0) for g in kv_heads]

    def logits_t(g, j):
        start = pl.multiple_of(j * bk, bk)
        return _dot_nt(k_ref[pl.ds(start, bk), ksl[g]], qs[g]).reshape(bk // sub, sub, rows)

    def pv_t(g, j, pt):
        return _dot(vt_ref[j, ksl[g], :], pt.reshape(bk, rows).astype(BF16))

    def bounded_body(j, carry):
        out = []
        for g, (l, acc) in zip(kv_heads, carry):
            pt = jnp.exp2(logits_t(g, j))
            out.append((l + jnp.sum(pt, axis=0), acc + pv_t(g, j, pt)))
        return tuple(out)

    def online_body(j, carry):
        out = []
        for g, (m, l, acc) in zip(kv_heads, carry):
            st = logits_t(g, j)
            m_new = jnp.maximum(m, jnp.max(jnp.max(st, axis=0), axis=0, keepdims=True))
            alpha = jnp.exp2(m - m_new)
            pt = jnp.exp2(st - m_new[None])
            out.append((m_new, alpha * l + jnp.sum(pt, axis=0), alpha[:1] * acc + pv_t(g, j, pt)))
        return tuple(out)

    if bounded:
        res = lax.fori_loop(0, n_kv, bounded_body, ((l0, acc0),) * B_KV_HEADS, unroll=min(n_kv, KV_UNROLL))
    else:
        m0 = jnp.full((sub, rows), NEG_INF, F32)
        res = [r[1:] for r in lax.fori_loop(0, n_kv, online_body, ((m0, l0, acc0),) * B_KV_HEADS)]
    for g, (l, acc) in zip(kv_heads, res):
        o = (acc * (1.0 / jnp.sum(l, axis=0, keepdims=True))).T
        for h in range(group):
            c0 = (g * group + h) * HEAD_DIM
            obuf[:, c0:c0 + HEAD_DIM] = o[h * bq:(h + 1) * bq]


def _global_kernel(q_ref, k_ref, vt_ref, g_ref, o_ref, obuf, *, bounded):
    _global_heads(q_ref, k_ref, vt_ref, obuf, bounded)
    o_ref[...] = _rms(obuf[...], g_ref[...]).astype(o_ref.dtype)


def _logits_bounded(qg, kg):
    bound = (HEAD_DIM ** 0.5) * jnp.max(jnp.abs(qg)) * jnp.max(jnp.abs(kg)) * ROUNDING_SLACK
    return bound <= SOFTMAX_BOUND_LIMIT


def _global_attention(q, k, vt, gb, bounded, bq):
    b, seq, _ = q.shape
    n_kv, _, bk = vt.shape[1:]
    tile = lambda bi, i: (bi, i, 0)
    fixed = lambda bi, i: (0, 0)

    def call(is_bounded):
        return pl.pallas_call(
            functools.partial(_global_kernel, bounded=is_bounded),
            grid=(b, seq // bq),
            in_specs=[pl.BlockSpec((None, bq, B_Q), tile),
                      _resident((None, seq, B_KV), lambda bi, i: (bi, 0, 0)),
                      _resident((None, n_kv, B_KV, bk), lambda bi, i: (bi, 0, 0, 0)),
                      _resident((1, B_Q), fixed)],
            out_specs=pl.BlockSpec((None, bq, B_Q), tile),
            out_shape=jax.ShapeDtypeStruct((b, seq, B_Q), BF16),
            scratch_shapes=[pltpu.VMEM((bq, B_Q), F32)],
            compiler_params=_params("parallel", "parallel"),
            name="global_attn_bounded" if is_bounded else "global_attn_online",
        )

    return lax.cond(bounded, call(True), call(False), q, k, vt, gb.reshape(1, B_Q))


def _outproj_kernel(oa_ref, ob_ref, om_ref, x_ref, w_ref, x1_ref):
    merged = jnp.concatenate([oa_ref[...], ob_ref[...], om_ref[...]], axis=1)
    x1_ref[...] = x_ref[...] + _dot(merged, w_ref[...])


def _outproj(oa, ob, om, x2, w_bf, tm):
    tokens, d = x2.shape
    row = lambda i: (i, 0)
    fixed = lambda i: (0, 0)
    return pl.pallas_call(
        _outproj_kernel,
        grid=(tokens // tm,),
        in_specs=[pl.BlockSpec((tm, A_Q), row), pl.BlockSpec((tm, B_Q), row), pl.BlockSpec((tm, M_Q), row),
                  pl.BlockSpec((tm, d), row), _resident(w_bf.shape, fixed)],
        out_specs=pl.BlockSpec((tm, d), row),
        out_shape=jax.ShapeDtypeStruct((tokens, d), F32),
        compiler_params=_params("parallel"),
        name="outproj",
    )(oa, ob, om, x2, w_bf)


def _ffn_kernel(x1_ref, gf_ref, wg_ref, wu_ref, wd_ref, gl_ref, o_ref, h_scr):
    j = pl.program_id(1)
    last = pl.num_programs(1) - 1
    n_blocks = x1_ref.shape[0] // ROW_BLOCK
    rows = lambda r: slice(r * ROW_BLOCK, (r + 1) * ROW_BLOCK)

    def chunk(first, final):
        def gate_up(r):
            if first:
                h_scr[rows(r)] = _rms(x1_ref[rows(r)], gf_ref[...]).astype(h_scr.dtype)
            h = h_scr[rows(r)]
            return _dot(h, wg_ref[...]), _dot(h, wu_ref[...])

        pending = gate_up(0)
        for r in range(n_blocks):
            ahead = gate_up(r + 1) if r + 1 < n_blocks else None
            gate, up = pending
            act = (gate * jax.nn.sigmoid(gate) * up).astype(BF16)
            acc = (x1_ref[rows(r)] if first else o_ref[rows(r)]) + _dot(act, wd_ref[...])
            o_ref[rows(r)] = _rms(acc, gl_ref[...]) if final else acc
            pending = ahead

    pl.when(j == 0)(lambda: chunk(True, False))
    pl.when(jnp.logical_and(j > 0, j < last))(lambda: chunk(False, False))
    pl.when(j == last)(lambda: chunk(False, True))


def _ffn(x1, gf, w_gu_bf, w_dn_bf, gl, tm, tf):
    tokens, d = x1.shape
    d_ff = w_dn_bf.shape[0]
    nf = d_ff // tf
    assert nf >= 2 and tm % ROW_BLOCK == 0
    row = lambda i, j: (i, 0)
    fixed = lambda i, j: (0, 0)
    return pl.pallas_call(
        _ffn_kernel,
        grid=(tokens // tm, nf),
        in_specs=[pl.BlockSpec((tm, d), row), _resident((1, d), fixed),
                  pl.BlockSpec((d, tf), lambda i, j: (0, j)), pl.BlockSpec((d, tf), lambda i, j: (0, j + nf)),
                  pl.BlockSpec((tf, d), lambda i, j: (j, 0)), _resident((1, d), fixed)],
        out_specs=pl.BlockSpec((tm, d), row),
        out_shape=jax.ShapeDtypeStruct((tokens, d), F32),
        scratch_shapes=[pltpu.VMEM((tm, d), BF16)],
        compiler_params=_params("parallel", "arbitrary"),
        name="ffn",
    )(x1, gf.reshape(1, d), w_gu_bf, w_gu_bf, w_dn_bf, gl.reshape(1, d))


def _tile(n, pref):
    t = min(pref, n)
    assert n % t == 0, (n, t)
    return t


def _tiles(seq, d_ff):
    return dict(
        proj=_tile(seq, 1024),
        local=_tile(seq, 1024),
        global_q=_tile(seq, 512),
        global_kv=_tile(seq, 4096),
        ffn=_tile(seq, 1024),
        ffn_cols=_tile(d_ff, 512),
    )


def _layer(x, mem, rope, norm_mix_g, norm_mem_g, w_in, w_mem_kv, sink_a, q_norm_b_g, k_norm_b_g,
           out_norm_g, w_out, norm_ffn_g, w_gate_up, w_down, final_g):
    b, seq, d = x.shape
    tokens = b * seq
    assert seq % GRID_W == 0 and seq % WINDOW == 0
    x2 = x.reshape(tokens, d)
    t = _tiles(seq, w_down.shape[0])

    qa, ka, va, qb, kb, vbt, qm = _inproj(x2, seq, norm_mix_g, w_in.astype(BF16), q_norm_b_g, k_norm_b_g,
                                          rope, t["proj"], t["global_kv"])
    km, vm = _memkv(mem, norm_mem_g, w_mem_kv.astype(BF16))

    r3 = lambda a: a.reshape(b, seq, a.shape[-1])
    oa, om = _local_attention(r3(qa), r3(ka), r3(va), r3(qm), km, vm, sink_a,
                              out_norm_g[:A_Q], out_norm_g[A_Q + B_Q:], t["local"])
    vbt = vbt.reshape(b, seq // t["global_kv"], B_KV, t["global_kv"])
    ob = _global_attention(r3(qb), r3(kb), vbt, out_norm_g[A_Q:A_Q + B_Q],
                           _logits_bounded(q_norm_b_g, k_norm_b_g), t["global_q"])

    x1 = _outproj(oa.reshape(tokens, A_Q), ob.reshape(tokens, B_Q), om.reshape(tokens, M_Q), x2,
                  w_out.astype(BF16), t["proj"])
    y = _ffn(x1, norm_ffn_g, w_gate_up.astype(BF16), w_down.astype(BF16), final_g, t["ffn"], t["ffn_cols"])
    return y.reshape(b, seq, d)


def kernel(x_prompt, x_sample, mem_prompt, mem_sample, norm_mix_g, norm_mem_g, w_in, w_mem_kv, sink_a,
           q_norm_b_g, k_norm_b_g, out_norm_g, w_out, norm_ffn_g, w_gate_up, w_down, norm_final_g):
    depth = w_in.shape[0]
    assert depth == 1, "final norm is fused into the last layer's ffn kernel"
    args = (norm_mix_g[0], norm_mem_g[0], w_in[0], w_mem_kv[0], sink_a[0], q_norm_b_g[0], k_norm_b_g[0],
            out_norm_g[0], w_out[0], norm_ffn_g[0], w_gate_up[0], w_down[0], norm_final_g)
    seq_max = max(x_prompt.shape[1], x_sample.shape[1])
    tm = _tiles(seq_max, w_down.shape[1])["proj"]
    assert _tiles(min(x_prompt.shape[1], x_sample.shape[1]), w_down.shape[1])["proj"] == tm
    rope = _rope_tables(seq_max, tm)
    return (_layer(x_prompt, mem_prompt, rope, *args), _layer(x_sample, mem_sample, rope, *args))
```

```python
import functools

import jax
import jax.numpy as jnp
from jax import lax
from jax.experimental import pallas as pl
from jax.experimental.pallas import tpu as pltpu

HEAD_DIM = 128
A_HEADS, A_KV_HEADS = 8, 2
B_HEADS, B_KV_HEADS = 4, 2
M_HEADS = 4
WINDOW = 128
GRID_W = 64
ROPE_THETA = 10000.0
NORM_EPS = 1e-6
NEG_INF = -1e30
A_Q, A_KV = A_HEADS * HEAD_DIM, A_KV_HEADS * HEAD_DIM
B_Q, B_KV = B_HEADS * HEAD_DIM, B_KV_HEADS * HEAD_DIM
M_Q = M_HEADS * HEAD_DIM
LOG2E = 1.4426950408889634
Q_SCALE = HEAD_DIM ** -0.5 * LOG2E
SOFTMAX_BOUND_LIMIT = 32.0
ROUNDING_SLACK = 1.02
KV_UNROLL = 2
ROW_BLOCK = 256

BF16 = jnp.bfloat16
F32 = jnp.float32
F32_SUBLANES = 8

VMEM_LIMIT_BYTES = 56 * 1024 * 1024


def _params(*sem):
    return pltpu.CompilerParams(dimension_semantics=sem, vmem_limit_bytes=VMEM_LIMIT_BYTES)


def _resident(shape, index_map):
    return pl.BlockSpec(shape, index_map, pipeline_mode=pl.Buffered(1))


def _rms(x, g):
    ms = jnp.mean(x * x, axis=-1, keepdims=True)
    return x * lax.rsqrt(ms + NORM_EPS) * g


def _dot(a, b):
    return jnp.dot(a, b, preferred_element_type=F32)


def _dot_nt(a, b):
    return lax.dot_general(a, b, (((1,), (1,)), ((), ())), preferred_element_type=F32)


def _inproj_kernel(x_ref, g_ref, w_ref, at_ref, ao_ref, br_ref, bc_ref, qg_ref, kg_ref,
                   qa_ref, ka_ref, va_ref, qb_ref, kb_ref, vbt_ref, qm_ref):
    h = _rms(x_ref[...], g_ref[...]).astype(BF16)
    tm = x_ref.shape[0]
    cos_t, sin_t, scos_t, ssin_t = (at_ref[i:i + 1, :] for i in range(4))
    cos_o, sin_o = ao_ref[0], ao_ref[1]
    ca = cos_t * cos_o - sin_t * sin_o
    sa = ssin_t * cos_o + scos_t * sin_o
    n_r = tm // GRID_W

    def axial(k):
        per_row = jnp.broadcast_to(br_ref[k][:, None, :], (n_r, GRID_W, HEAD_DIM)).reshape(tm, HEAD_DIM)
        return per_row + jnp.concatenate([bc_ref[k]] * n_r, axis=0)

    cb, sb = axial(0), axial(1)
    lane = lax.broadcasted_iota(jnp.int32, ca.shape, 1)
    low_quarter = (lane % (HEAD_DIM // 2)) < (HEAD_DIM // 4)

    def rope_a(t):
        return t * ca + pltpu.roll(t, HEAD_DIM // 2, 1) * sa

    def rope_b(t):
        partner = jnp.where(low_quarter, pltpu.roll(t, HEAD_DIM - HEAD_DIM // 4, 1),
                            pltpu.roll(t, HEAD_DIM // 4, 1))
        return t * cb + partner * sb

    def heads(t, n, fn, out_ref):
        for i in range(n):
            sl = slice(i * HEAD_DIM, (i + 1) * HEAD_DIM)
            out_ref[:, sl] = fn(t[:, sl]).astype(out_ref.dtype)

    widths = (A_Q, A_KV, A_KV, B_Q, B_KV, B_KV, M_Q)
    starts = [sum(widths[:i]) for i in range(len(widths))]

    def section(i):
        return _dot(h, w_ref[:, starts[i]:starts[i] + widths[i]])

    qg, kg = qg_ref[...], kg_ref[...]
    heads(section(3), B_HEADS, lambda t: rope_b(_rms(t, qg)) * Q_SCALE, qb_ref)
    heads(section(4), B_KV_HEADS, lambda t: rope_b(_rms(t, kg)), kb_ref)
    vbt_ref[...] = section(5).astype(vbt_ref.dtype).T
    heads(section(0), A_HEADS, lambda t: rope_a(t) * Q_SCALE, qa_ref)
    heads(section(1), A_KV_HEADS, rope_a, ka_ref)
    qm_ref[...] = (section(6) * Q_SCALE).astype(qm_ref.dtype)
    va_ref[...] = section(2).astype(va_ref.dtype)


def _rope_tables(seq, tm):
    def angles(n, dim):
        inv = ROPE_THETA ** (-(jnp.arange(0, dim, 2, dtype=F32) / dim))
        return jnp.arange(n, dtype=F32)[:, None] * jnp.tile(inv, 2)[None, :]

    def sign(dim):
        return jnp.where(jnp.arange(dim) < dim // 2, -1.0, 1.0).astype(F32)

    assert seq % tm == 0 and tm % (F32_SUBLANES * GRID_W) == 0
    start = angles(seq // tm, HEAD_DIM) * tm
    sg = sign(HEAD_DIM)
    a_tile = jnp.stack([jnp.cos(start), jnp.sin(start), sg * jnp.cos(start), sg * jnp.sin(start)], axis=1)
    a_tile = jnp.pad(a_tile, ((0, 0), (0, F32_SUBLANES - a_tile.shape[1]), (0, 0)))
    off = angles(tm, HEAD_DIM)
    a_off = jnp.stack([jnp.cos(off), jnp.sin(off)])

    half = HEAD_DIM // 2
    zeros = lambda n: jnp.zeros((n, half), F32)
    rows, cols = angles(seq // GRID_W, half), angles(GRID_W, half)
    b_row = jnp.stack([jnp.concatenate([jnp.cos(rows), zeros(rows.shape[0])], axis=1),
                       jnp.concatenate([jnp.sin(rows) * sign(half), zeros(rows.shape[0])], axis=1)])
    b_col = jnp.stack([jnp.concatenate([zeros(GRID_W), jnp.cos(cols)], axis=1),
                       jnp.concatenate([zeros(GRID_W), jnp.sin(cols) * sign(half)], axis=1)])
    return a_tile, a_off, b_row, b_col


def _inproj(x2, seq, g, w_bf, qg, kg, rope, tm, kv_block):
    tokens, d = x2.shape
    per_seq = seq // tm
    per_kv = kv_block // tm
    assert kv_block % tm == 0 and seq % kv_block == 0
    a_tile, a_off, b_row, b_col = rope
    assert a_off.shape[1] == tm
    row = lambda i: (i, 0)
    fixed = lambda i: (0, 0)
    widths = (A_Q, A_KV, A_KV, B_Q, B_KV, B_KV, M_Q)
    out_specs = [pl.BlockSpec((tm, w), row) for w in widths]
    out_shape = [jax.ShapeDtypeStruct((tokens, w), BF16) for w in widths]
    out_specs[5] = pl.BlockSpec((None, B_KV, tm), lambda i: (i // per_kv, 0, i % per_kv))
    out_shape[5] = jax.ShapeDtypeStruct((tokens // kv_block, B_KV, kv_block), BF16)
    return pl.pallas_call(
        _inproj_kernel,
        grid=(tokens // tm,),
        in_specs=[pl.BlockSpec((tm, d), row),
                  _resident((1, d), fixed),
                  _resident(w_bf.shape, fixed),
                  pl.BlockSpec((None, F32_SUBLANES, HEAD_DIM), lambda i: (i % per_seq, 0, 0)),
                  _resident(a_off.shape, lambda i: (0, 0, 0)),
                  pl.BlockSpec((2, tm // GRID_W, HEAD_DIM), lambda i: (0, i % per_seq, 0)),
                  _resident(b_col.shape, lambda i: (0, 0, 0)),
                  _resident((1, HEAD_DIM), fixed), _resident((1, HEAD_DIM), fixed)],
        out_specs=out_specs,
        out_shape=out_shape,
        compiler_params=_params("parallel"),
        name="inproj",
    )(x2, g.reshape(1, d), w_bf, a_tile, a_off, b_row, b_col, qg.reshape(1, HEAD_DIM), kg.reshape(1, HEAD_DIM))


def _memkv_kernel(mem_ref, g_ref, w_ref, km_ref, vm_ref):
    h = _rms(mem_ref[...], g_ref[...]).astype(BF16)
    kv = _dot(h, w_ref[...])
    km_ref[...] = kv[:, :M_Q].astype(km_ref.dtype)
    vm_ref[...] = kv[:, M_Q:].astype(vm_ref.dtype)


def _memkv(mem, g, w_bf):
    b, n, d = mem.shape
    blk = lambda i: (i, 0, 0)
    fixed = lambda i: (0, 0)
    return pl.pallas_call(
        _memkv_kernel,
        grid=(b,),
        in_specs=[pl.BlockSpec((None, n, d), blk), _resident((1, d), fixed), _resident(w_bf.shape, fixed)],
        out_specs=[pl.BlockSpec((None, n, M_Q), blk), pl.BlockSpec((None, n, M_Q), blk)],
        out_shape=[jax.ShapeDtypeStruct((b, n, M_Q), BF16)] * 2,
        compiler_params=_params("parallel"),
        name="memkv",
    )(mem, g.reshape(1, d), w_bf)


def _local_kernel(sink_ref, qa_ref, kp_ref, kc_ref, kn_ref, vp_ref, vc_ref, vn_ref,
                  qm_ref, km_ref, vm_ref, ga_ref, gm_ref, oa_ref, om_ref,
                  kbuf, vbuf, vmbuf, sbuf, smbuf, obuf, mbuf):
    bq = qa_ref.shape[0]
    nsub = bq // WINDOW
    i = pl.program_id(1)
    first = i == 0
    last = i == pl.num_programs(1) - 1
    group = A_HEADS // A_KV_HEADS
    cols = 3 * WINDOW
    two = 2 * HEAD_DIM

    kbuf[0:WINDOW] = kp_ref[...]
    kbuf[WINDOW:WINDOW + bq] = kc_ref[...]
    kbuf[WINDOW + bq:] = kn_ref[...]
    for g in range(A_KV_HEADS):
        sl = slice(g * HEAD_DIM, (g + 1) * HEAD_DIM)
        vbuf[0:WINDOW, g * two:g * two + HEAD_DIM] = vp_ref[:, sl]
        vbuf[WINDOW:WINDOW + bq, g * two:g * two + HEAD_DIM] = vc_ref[:, sl]
        vbuf[WINDOW + bq:, g * two:g * two + HEAD_DIM] = vn_ref[:, sl]
        vbuf[:, g * two + HEAD_DIM:(g + 1) * two] = jnp.ones((bq + 2 * WINDOW, HEAD_DIM), vbuf.dtype)
    for h in range(M_HEADS):
        vmbuf[:, h * two:h * two + HEAD_DIM] = vm_ref[:, h * HEAD_DIM:(h + 1) * HEAD_DIM]
        vmbuf[:, h * two + HEAD_DIM:(h + 1) * two] = jnp.ones((vm_ref.shape[0], HEAD_DIM), vmbuf.dtype)

    for j in range(nsub):
        for g in range(A_KV_HEADS):
            q = jnp.concatenate(
                [qa_ref[j * WINDOW:(j + 1) * WINDOW, (g * group + h) * HEAD_DIM:(g * group + h + 1) * HEAD_DIM]
                 for h in range(group)], axis=0)
            sbuf[j * A_KV_HEADS + g] = _dot_nt(q, kbuf[j * WINDOW:j * WINDOW + cols, g * HEAD_DIM:(g + 1) * HEAD_DIM])
    for h in range(M_HEADS):
        sl = slice(h * HEAD_DIM, (h + 1) * HEAD_DIM)
        smbuf[h] = _dot_nt(qm_ref[:, sl], km_ref[:, sl])

    qi = lax.broadcasted_iota(jnp.int32, (WINDOW, WINDOW), 0)
    kc = lax.broadcasted_iota(jnp.int32, (WINDOW, WINDOW), 1)
    neg = jnp.float32(NEG_INF)
    tile_heads = lambda t: jnp.concatenate([t] * group, axis=0)
    bias_left = jnp.where(kc >= qi, 0.0, neg)
    bias_right = jnp.where(kc <= qi, 0.0, neg)
    bias_left_first = tile_heads(jnp.where(first, neg, bias_left))
    bias_right_last = tile_heads(jnp.where(last, neg, bias_right))
    bias_left, bias_right = tile_heads(bias_left), tile_heads(bias_right)

    def softmax_pv(chunks, m, v):
        p = jnp.concatenate([jnp.exp2(c - m) for c in chunks], axis=1).astype(BF16)
        pv = _dot(p, v)
        return pv[:, :HEAD_DIM], pv[:, HEAD_DIM:]

    for j in range(nsub):
        for g in range(A_KV_HEADS):
            s = sbuf[j * A_KV_HEADS + g]
            chunks = [s[:, 0:WINDOW] + (bias_left_first if j == 0 else bias_left),
                      s[:, WINDOW:2 * WINDOW],
                      s[:, 2 * WINDOW:] + (bias_right_last if j == nsub - 1 else bias_right)]
            sink = jnp.concatenate(
                [jnp.full((WINDOW, HEAD_DIM), sink_ref[g * group + h] * LOG2E, F32) for h in range(group)], axis=0)
            m = jnp.maximum(jnp.max(functools.reduce(jnp.maximum, chunks), axis=-1, keepdims=True), sink)
            num, den = softmax_pv(chunks, m, vbuf[j * WINDOW:j * WINDOW + cols, g * two:(g + 1) * two])
            o = num * (1.0 / (den + jnp.exp2(sink - m)))
            for h in range(group):
                c0 = (g * group + h) * HEAD_DIM
                obuf[j * WINDOW:(j + 1) * WINDOW, c0:c0 + HEAD_DIM] = o[h * WINDOW:(h + 1) * WINDOW]
    oa_ref[...] = _rms(obuf[...], ga_ref[...]).astype(oa_ref.dtype)

    for h in range(M_HEADS):
        s = smbuf[h]
        chunks = [s[:, c * HEAD_DIM:(c + 1) * HEAD_DIM] for c in range(s.shape[1] // HEAD_DIM)]
        m = jnp.max(functools.reduce(jnp.maximum, chunks), axis=-1, keepdims=True)
        m = jnp.broadcast_to(m, chunks[0].shape)
        num, den = softmax_pv(chunks, m, vmbuf[:, h * two:(h + 1) * two])
        mbuf[:, h * HEAD_DIM:(h + 1) * HEAD_DIM] = num * (1.0 / den)
    om_ref[...] = _rms(mbuf[...], gm_ref[...]).astype(om_ref.dtype)


def _local_attention(qa, ka, va, qm, km, vm, sink, ga, gm, bq):
    b, seq, _ = qa.shape
    nt = seq // bq
    sub = bq // WINDOW
    nblk = seq // WINDOW
    n_mem = km.shape[1]
    tile = lambda bi, i: (bi, i, 0)
    prev = lambda bi, i: (bi, jnp.maximum(i * sub - 1, 0), 0)
    nxt = lambda bi, i: (bi, jnp.minimum((i + 1) * sub, nblk - 1), 0)
    per_b = lambda bi, i: (bi, 0, 0)
    fixed = lambda bi, i: (0, 0)
    kv_specs = [pl.BlockSpec((None, WINDOW, A_KV), prev), pl.BlockSpec((None, bq, A_KV), tile),
                pl.BlockSpec((None, WINDOW, A_KV), nxt)]
    return pl.pallas_call(
        _local_kernel,
        grid=(b, nt),
        in_specs=[pl.BlockSpec(memory_space=pltpu.SMEM),
                  pl.BlockSpec((None, bq, A_Q), tile)] + kv_specs + kv_specs + [
                  pl.BlockSpec((None, bq, M_Q), tile),
                  pl.BlockSpec((None, n_mem, M_Q), per_b), pl.BlockSpec((None, n_mem, M_Q), per_b),
                  _resident((1, A_Q), fixed), _resident((1, M_Q), fixed)],
        out_specs=[pl.BlockSpec((None, bq, A_Q), tile), pl.BlockSpec((None, bq, M_Q), tile)],
        out_shape=[jax.ShapeDtypeStruct((b, seq, A_Q), BF16), jax.ShapeDtypeStruct((b, seq, M_Q), BF16)],
        scratch_shapes=[pltpu.VMEM((bq + 2 * WINDOW, A_KV), BF16),
                        pltpu.VMEM((bq + 2 * WINDOW, 2 * A_KV), BF16),
                        pltpu.VMEM((n_mem, 2 * M_Q), BF16),
                        pltpu.VMEM((sub * A_KV_HEADS, (A_HEADS // A_KV_HEADS) * WINDOW, 3 * WINDOW), F32),
                        pltpu.VMEM((M_HEADS, bq, n_mem), F32),
                        pltpu.VMEM((bq, A_Q), F32), pltpu.VMEM((bq, M_Q), F32)],
        compiler_params=_params("parallel", "parallel"),
        name="local_attn",
    )(sink, qa, ka, ka, ka, va, va, va, qm, km, vm, ga.reshape(1, A_Q), gm.reshape(1, M_Q))


def _global_heads(q_ref, k_ref, vt_ref, obuf, bounded):
    bq = q_ref.shape[0]
    n_kv, _, bk = vt_ref.shape
    group = B_HEADS // B_KV_HEADS
    rows = group * bq
    sub = F32_SUBLANES
    l0 = jnp.zeros((sub, rows), F32)
    acc0 = jnp.zeros((HEAD_DIM, rows), F32)
    kv_heads = range(B_KV_HEADS)
    ksl = [slice(g * HEAD_DIM, (g + 1) * HEAD_DIM) for g in kv_heads]
    qs = [jnp.concatenate([q_ref[:, (g * group + h) * HEAD_DIM:(g * group + h + 1) * HEAD_DIM]
                           for h in range(group)], axis=0) for g in kv_heads]

    def logits_t(g, j):
        start = pl.multiple_of(j * bk, bk)
        return _dot_nt(k_ref[pl.ds(start, bk), ksl[g]], qs[g]).reshape(bk // sub, sub, rows)

    def pv_t(g, j, pt):
        return _dot(vt_ref[j, ksl[g], :], pt.reshape(bk, rows).astype(BF16))

    def bounded_body(j, carry):
        out = []
        for g, (l, acc) in zip(kv_heads, carry):
            pt = jnp.exp2(logits_t(g, j))
            out.append((l + jnp.sum(pt, axis=0), acc + pv_t(g, j, pt)))
        return tuple(out)

    def online_body(j, carry):
        out = []
        for g, (m, l, acc) in zip(kv_heads, carry):
            st = logits_t(g, j)
            m_new = jnp.maximum(m, jnp.max(jnp.max(st, axis=0), axis=0, keepdims=True))
            alpha = jnp.exp2(m - m_new)
            pt = jnp.exp2(st - m_new[None])
            out.append((m_new, alpha * l + jnp.sum(pt, axis=0), alpha[:1] * acc + pv_t(g, j, pt)))
        return tuple(out)

    if bounded:
        res = lax.fori_loop(0, n_kv, bounded_body, ((l0, acc0),) * B_KV_HEADS, unroll=min(n_kv, KV_UNROLL))
    else:
        m0 = jnp.full((sub, rows), NEG_INF, F32)
        res = [r[1:] for r in lax.fori_loop(0, n_kv, online_body, ((m0, l0, acc0),) * B_KV_HEADS)]
    for g, (l, acc) in zip(kv_heads, res):
        o = (acc * (1.0 / jnp.sum(l, axis=0, keepdims=True))).T
        for h in range(group):
            c0 = (g * group + h) * HEAD_DIM
            obuf[:, c0:c0 + HEAD_DIM] = o[h * bq:(h + 1) * bq]


def _global_kernel(q_ref, k_ref, vt_ref, g_ref, o_ref, obuf, *, bounded):
    _global_heads(q_ref, k_ref, vt_ref, obuf, bounded)
    o_ref[...] = _rms(obuf[...], g_ref[...]).astype(o_ref.dtype)


def _logits_bounded(qg, kg):
    bound = (HEAD_DIM ** 0.5) * jnp.max(jnp.abs(qg)) * jnp.max(jnp.abs(kg)) * ROUNDING_SLACK
    return bound <= SOFTMAX_BOUND_LIMIT


def _global_attention(q, k, vt, gb, bounded, bq):
    b, seq, _ = q.shape
    n_kv, _, bk = vt.shape[1:]
    tile = lambda bi, i: (bi, i, 0)
    fixed = lambda bi, i: (0, 0)

    def call(is_bounded):
        return pl.pallas_call(
            functools.partial(_global_kernel, bounded=is_bounded),
            grid=(b, seq // bq),
            in_specs=[pl.BlockSpec((None, bq, B_Q), tile),
                      _resident((None, seq, B_KV), lambda bi, i: (bi, 0, 0)),
                      _resident((None, n_kv, B_KV, bk), lambda bi, i: (bi, 0, 0, 0)),
                      _resident((1, B_Q), fixed)],
            out_specs=pl.BlockSpec((None, bq, B_Q), tile),
            out_shape=jax.ShapeDtypeStruct((b, seq, B_Q), BF16),
            scratch_shapes=[pltpu.VMEM((bq, B_Q), F32)],
            compiler_params=_params("parallel", "parallel"),
            name="global_attn_bounded" if is_bounded else "global_attn_online",
        )

    return lax.cond(bounded, call(True), call(False), q, k, vt, gb.reshape(1, B_Q))


def _outproj_kernel(oa_ref, ob_ref, om_ref, x_ref, w_ref, x1_ref):
    merged = jnp.concatenate([oa_ref[...], ob_ref[...], om_ref[...]], axis=1)
    x1_ref[...] = x_ref[...] + _dot(merged, w_ref[...])


def _outproj(oa, ob, om, x2, w_bf, tm):
    tokens, d = x2.shape
    row = lambda i: (i, 0)
    fixed = lambda i: (0, 0)
    return pl.pallas_call(
        _outproj_kernel,
        grid=(tokens // tm,),
        in_specs=[pl.BlockSpec((tm, A_Q), row), pl.BlockSpec((tm, B_Q), row), pl.BlockSpec((tm, M_Q), row),
                  pl.BlockSpec((tm, d), row), _resident(w_bf.shape, fixed)],
        out_specs=pl.BlockSpec((tm, d), row),
        out_shape=jax.ShapeDtypeStruct((tokens, d), F32),
        compiler_params=_params("parallel"),
        name="outproj",
    )(oa, ob, om, x2, w_bf)


def _ffn_kernel(x1_ref, gf_ref, wg_ref, wu_ref, wd_ref, gl_ref, o_ref, h_scr):
    j = pl.program_id(1)
    last = pl.num_programs(1) - 1
    n_blocks = x1_ref.shape[0] // ROW_BLOCK
    rows = lambda r: slice(r * ROW_BLOCK, (r + 1) * ROW_BLOCK)

    def chunk(first, final):
        def gate_up(r):
            if first:
                h_scr[rows(r)] = _rms(x1_ref[rows(r)], gf_ref[...]).astype(h_scr.dtype)
            h = h_scr[rows(r)]
            return _dot(h, wg_ref[...]), _dot(h, wu_ref[...])

        pending = gate_up(0)
        for r in range(n_blocks):
            ahead = gate_up(r + 1) if r + 1 < n_blocks else None
            gate, up = pending
            act = (gate * jax.nn.sigmoid(gate) * up).astype(BF16)
            acc = (x1_ref[rows(r)] if first else o_ref[rows(r)]) + _dot(act, wd_ref[...])
            o_ref[rows(r)] = _rms(acc, gl_ref[...]) if final else acc
            pending = ahead

    pl.when(j == 0)(lambda: chunk(True, False))
    pl.when(jnp.logical_and(j > 0, j < last))(lambda: chunk(False, False))
    pl.when(j == last)(lambda: chunk(False, True))


def _ffn_weight_chunks(w_gate_up, tf):
    d, two_dff = w_gate_up.shape
    nf = two_dff // 2 // tf
    return w_gate_up.reshape(d, 2, nf, tf).transpose(1, 2, 0, 3).astype(BF16)


def _ffn(x1, gf, w_gu_chunks, w_dn_bf, gl, tm):
    tokens, d = x1.shape
    _, nf, _, tf = w_gu_chunks.shape
    assert nf >= 2 and tm % ROW_BLOCK == 0
    row = lambda i, j: (i, 0)
    fixed = lambda i, j: (0, 0)
    return pl.pallas_call(
        _ffn_kernel,
        grid=(tokens // tm, nf),
        in_specs=[pl.BlockSpec((tm, d), row), _resident((1, d), fixed),
                  pl.BlockSpec((None, None, d, tf), lambda i, j: (0, j, 0, 0)),
                  pl.BlockSpec((None, None, d, tf), lambda i, j: (1, j, 0, 0)),
                  pl.BlockSpec((tf, d), lambda i, j: (j, 0)), _resident((1, d), fixed)],
        out_specs=pl.BlockSpec((tm, d), row),
        out_shape=jax.ShapeDtypeStruct((tokens, d), F32),
        scratch_shapes=[pltpu.VMEM((tm, d), BF16)],
        compiler_params=_params("parallel", "arbitrary"),
        name="ffn",
    )(x1, gf.reshape(1, d), w_gu_chunks, w_gu_chunks, w_dn_bf, gl.reshape(1, d))


def _tile(n, pref):
    t = min(pref, n)
    assert n % t == 0, (n, t)
    return t


def _tiles(seq, d_ff):
    return dict(
        proj=_tile(seq, 1024),
        local=_tile(seq, 1024),
        global_q=_tile(seq, 512),
        global_kv=_tile(seq, 4096),
        ffn=_tile(seq, 1024),
        ffn_cols=_tile(d_ff, 512),
    )


def _layer(x, mem, rope, norm_mix_g, norm_mem_g, w_in, w_mem_kv, sink_a, q_norm_b_g, k_norm_b_g,
           out_norm_g, w_out, norm_ffn_g, w_gate_up, w_down, final_g):
    b, seq, d = x.shape
    tokens = b * seq
    assert seq % GRID_W == 0 and seq % WINDOW == 0
    x2 = x.reshape(tokens, d)
    t = _tiles(seq, w_down.shape[0])

    qa, ka, va, qb, kb, vbt, qm = _inproj(x2, seq, norm_mix_g, w_in.astype(BF16), q_norm_b_g, k_norm_b_g,
                                          rope, t["proj"], t["global_kv"])
    km, vm = _memkv(mem, norm_mem_g, w_mem_kv.astype(BF16))

    r3 = lambda a: a.reshape(b, seq, a.shape[-1])
    oa, om = _local_attention(r3(qa), r3(ka), r3(va), r3(qm), km, vm, sink_a,
                              out_norm_g[:A_Q], out_norm_g[A_Q + B_Q:], t["local"])
    vbt = vbt.reshape(b, seq // t["global_kv"], B_KV, t["global_kv"])
    ob = _global_attention(r3(qb), r3(kb), vbt, out_norm_g[A_Q:A_Q + B_Q],
                           _logits_bounded(q_norm_b_g, k_norm_b_g), t["global_q"])

    x1 = _outproj(oa.reshape(tokens, A_Q), ob.reshape(tokens, B_Q), om.reshape(tokens, M_Q), x2,
                  w_out.astype(BF16), t["proj"])
    y = _ffn(x1, norm_ffn_g, _ffn_weight_chunks(w_gate_up, t["ffn_cols"]), w_down.astype(BF16), final_g, t["ffn"])
    return y.reshape(b, seq, d)


def kernel(x_prompt, x_sample, mem_prompt, mem_sample, norm_mix_g, norm_mem_g, w_in, w_mem_kv, sink_a,
           q_norm_b_g, k_norm_b_g, out_norm_g, w_out, norm_ffn_g, w_gate_up, w_down, norm_final_g):
    depth = w_in.shape[0]
    assert depth == 1, "final norm is fused into the last layer's ffn kernel"
    args = (norm_mix_g[0], norm_mem_g[0], w_in[0], w_mem_kv[0], sink_a[0], q_norm_b_g[0], k_norm_b_g[0],
            out_norm_g[0], w_out[0], norm_ffn_g[0], w_gate_up[0], w_down[0], norm_final_g)
    seq_max = max(x_prompt.shape[1], x_sample.shape[1])
    tm = _tiles(seq_max, w_down.shape[1])["proj"]
    assert _tiles(min(x_prompt.shape[1], x_sample.shape[1]), w_down.shape[1])["proj"] == tm
    rope = _rope_tables(seq_max, tm)
    return (_layer(x_prompt, mem_prompt, rope, *args), _layer(x_sample, mem_sample, rope, *args))
```
